```python
import math
import jax, jax.numpy as jnp
from jax import lax
import numpy as np

D_MODEL = 2048
BATCH = 4
SEQ = 8192
DEPTH = 1

BLOCK = 128
WINDOW = 128
A_HEADS = 8
A_HEAD_DIM = 128
A_WIDTH = A_HEADS * A_HEAD_DIM
B_HEADS = 16
B_KV_HEADS = 2
HEAD_DIM = 64
B_WIDTH = B_HEADS * HEAD_DIM
KV_WIDTH = B_KV_HEADS * HEAD_DIM
MIX_WIDTH = A_WIDTH + B_WIDTH
IN_COLS = 2 * A_WIDTH + B_WIDTH + 2 * KV_WIDTH
D_FF = 5632
N_BUCKETS = 32
MAX_DISTANCE = 128
N_MOD = 9
EPS = 1e-6

kernel_name = "hybrid_gmlp_swa_macaron_adaln_layer"


def rms_norm(x, g):
    xf = x.astype(jnp.float32)
    y = xf * lax.rsqrt(jnp.mean(xf * xf, axis=-1, keepdims=True) + EPS)
    return (y * g.astype(jnp.float32)).astype(x.dtype)


def modulate(h, shift, scale):
    return h * (1 + scale[:, None, :]) + shift[:, None, :]


def swiglu(h, w1, w3, w2):
    return (jax.nn.silu(h @ w1) * (h @ w3)) @ w2


def t5_bucket(n):
    max_exact = N_BUCKETS // 2
    nf = jnp.maximum(n, 1).astype(jnp.float32)
    large = max_exact + (jnp.log(nf / max_exact) / math.log(MAX_DISTANCE / max_exact)
                         * (N_BUCKETS - max_exact)).astype(jnp.int32)
    large = jnp.minimum(large, N_BUCKETS - 1)
    return jnp.where(n < max_exact, n, large)


def gmlp_mixer(u, v, spatial_w, spatial_b, g_v):
    bsz, seq, _ = u.shape
    nb = seq // BLOCK
    u = u.reshape(bsz, nb, BLOCK, A_HEADS, A_HEAD_DIM)
    v = rms_norm(v.reshape(bsz, nb, BLOCK, A_HEADS, A_HEAD_DIM), g_v)
    causal = jnp.tril(jnp.ones((BLOCK, BLOCK), dtype=bool))
    w = jnp.where(causal[None], spatial_w, 0)
    mixed = jnp.einsum('hij,bnjhd->bnihd', w, v) + spatial_b.T[None, None, :, :, None]
    return (u * mixed).reshape(bsz, seq, A_WIDTH)


def swa_mixer(q, k, v, g_q, g_k, sinks, rel_bias):
    bsz, seq, _ = q.shape
    nb = seq // BLOCK
    grp = B_HEADS // B_KV_HEADS
    q = rms_norm(q.reshape(bsz, seq, B_HEADS, HEAD_DIM), g_q)
    k = rms_norm(k.reshape(bsz, seq, B_KV_HEADS, HEAD_DIM), g_k)
    q = q.reshape(bsz, nb, BLOCK, B_KV_HEADS, grp, HEAD_DIM)
    k = k.reshape(bsz, nb, BLOCK, B_KV_HEADS, HEAD_DIM)
    v = v.reshape(bsz, nb, BLOCK, B_KV_HEADS, HEAD_DIM)

    def band(t):
        prev = jnp.pad(t[:, :-1], ((0, 0), (1, 0), (0, 0), (0, 0), (0, 0)))
        return jnp.concatenate([prev, t], axis=2)

    kb, vb = band(k), band(v)
    s = jnp.einsum('bnikgd,bnjkd->bnkgij', q, kb).astype(jnp.float32) * (HEAD_DIM ** -0.5)

    qi = jnp.arange(BLOCK)[:, None]
    kj = jnp.arange(2 * BLOCK)[None, :]
    dist = qi + BLOCK - kj
    in_window = (dist >= 0) & (dist < WINDOW)
    bucket = t5_bucket(jnp.clip(dist, 0, None))
    bias = jnp.transpose(rel_bias[bucket], (2, 0, 1)).astype(jnp.float32)
    bias = bias.reshape(B_KV_HEADS, grp, BLOCK, 2 * BLOCK)
    key_pos = jnp.arange(nb)[:, None] * BLOCK - BLOCK + kj
    valid = in_window[None] & (key_pos >= 0)[:, None, :]

    s = jnp.where(valid[None, :, None, None], s + bias, -jnp.inf)
    sink = sinks.astype(jnp.float32).reshape(B_KV_HEADS, grp)[:, :, None, None]
    m = jnp.maximum(jnp.max(s, axis=-1, keepdims=True), sink)
    p = jnp.exp(s - m)
    w = p / (jnp.sum(p, axis=-1, keepdims=True) + jnp.exp(sink - m))
    o = jnp.einsum('bnkgij,bnjkd->bnikgd', w.astype(vb.dtype), vb)
    return o.reshape(bsz, seq, B_WIDTH)


def setup_inputs(seed: int = 0) -> dict:
    key = jax.random.key(seed)
    ks = jax.random.split(key, 24)
    f32 = jnp.float32
    nrm = lambda k, shape, s: jax.random.normal(k, shape, f32) * s
    gain = lambda k, shape: 1.0 + 0.02 * jax.random.normal(k, shape, f32)
    L, D = DEPTH, D_MODEL
    return {
        "x": nrm(ks[0], (BATCH, SEQ, D), 1.0),
        "c": nrm(ks[1], (BATCH, D), 1.0),
        "w_ada": nrm(ks[2], (L, D, N_MOD * D), 0.5 * D ** -0.5),
        "b_ada": nrm(ks[3], (L, N_MOD * D), 0.01),
        "g_ffn1": gain(ks[4], (L, D)),
        "w1_ffn1": nrm(ks[5], (L, D, D_FF), D ** -0.5),
        "w3_ffn1": nrm(ks[6], (L, D, D_FF), D ** -0.5),
        "w2_ffn1": nrm(ks[7], (L, D_FF, D), D_FF ** -0.5),
        "g_mix": gain(ks[8], (L, D)),
        "w_in": nrm(ks[9], (L, D, IN_COLS), D ** -0.5),
        "spatial_w": nrm(ks[10], (L, A_HEADS, BLOCK, BLOCK), BLOCK ** -0.5),
        "spatial_b": 1.0 + nrm(ks[11], (L, A_HEADS, BLOCK), 0.01),
        "g_v": gain(ks[12], (L, A_HEADS, A_HEAD_DIM)),
        "g_q": gain(ks[13], (L, HEAD_DIM)),
        "g_k": gain(ks[14], (L, HEAD_DIM)),
        "sinks": nrm(ks[15], (L, B_HEADS), 1.0),
        "rel_bias": nrm(ks[16], (N_BUCKETS, B_HEADS), 0.5),
        "w_out": nrm(ks[17], (L, MIX_WIDTH, D), MIX_WIDTH ** -0.5),
        "g_ffn2": gain(ks[18], (L, D)),
        "w1_ffn2": nrm(ks[19], (L, D, D_FF), D ** -0.5),
        "w3_ffn2": nrm(ks[20], (L, D, D_FF), D ** -0.5),
        "w2_ffn2": nrm(ks[21], (L, D_FF, D), D_FF ** -0.5),
    }


def reference(x, c, w_ada, b_ada, g_ffn1, w1_ffn1, w3_ffn1, w2_ffn1, g_mix, w_in,
              spatial_w, spatial_b, g_v, g_q, g_k, sinks, rel_bias, w_out,
              g_ffn2, w1_ffn2, w3_ffn2, w2_ffn2):
    c_act = jax.nn.silu(c)
    split_pts = [A_WIDTH, 2 * A_WIDTH, 2 * A_WIDTH + B_WIDTH, 2 * A_WIDTH + B_WIDTH + KV_WIDTH]
    for l in range(DEPTH):
        mod = c_act @ w_ada[l] + b_ada[l]
        sh1, sc1, gt1, sh2, sc2, gt2, sh3, sc3, gt3 = jnp.split(mod, N_MOD, axis=-1)

        h = modulate(rms_norm(x, g_ffn1[l]), sh1, sc1)
        x = x + 0.5 * gt1[:, None, :] * swiglu(h, w1_ffn1[l], w3_ffn1[l], w2_ffn1[l])

        h = modulate(rms_norm(x, g_mix[l]), sh2, sc2)
        z = h @ w_in[l]
        za_u, za_v, zq, zk, zv = jnp.split(z, split_pts, axis=-1)
        ya = gmlp_mixer(jax.nn.gelu(za_u, approximate=False), jax.nn.gelu(za_v, approximate=False),
                        spatial_w[l], spatial_b[l], g_v[l])
        yb = swa_mixer(zq, zk, zv, g_q[l], g_k[l], sinks[l], rel_bias)
        y = jnp.concatenate([ya, yb], axis=-1) @ w_out[l]
        x = x + gt2[:, None, :] * y

        h = modulate(rms_norm(x, g_ffn2[l]), sh3, sc3)
        x = x + 0.5 * gt3[:, None, :] * swiglu(h, w1_ffn2[l], w3_ffn2[l], w2_ffn2[l])
    return x
```

```python
import functools
import math

import jax
import jax.numpy as jnp
import numpy as np
from jax import lax
from jax.experimental import pallas as pl
from jax.experimental.pallas import tpu as pltpu

BLOCK = 128
A_HEADS = 8
A_HEAD_DIM = 128
A_WIDTH = A_HEADS * A_HEAD_DIM
B_HEADS = 16
B_KV_HEADS = 2
GROUP = B_HEADS // B_KV_HEADS
HEAD_DIM = 64
B_WIDTH = B_HEADS * HEAD_DIM
KV_WIDTH = B_KV_HEADS * HEAD_DIM
QKV_WIDTH = B_WIDTH + 2 * KV_WIDTH
N_BUCKETS = 32
MAX_DISTANCE = 128
N_MOD = 9
EPS = 1e-6
MASKED = -1e30

V7X_VMEM_LIMIT_BYTES = 60000 * 1024

F32 = jnp.float32
BF16 = jnp.bfloat16


def _dot(a, b):
    return jnp.dot(a, b, preferred_element_type=F32)


def _dot_nt(a, b):
    return lax.dot_general(a, b, (((1,), (1,)), ((), ())), preferred_element_type=F32)


def _dot_tn(a, b):
    return lax.dot_general(a, b, (((0,), (0,)), ((), ())), preferred_element_type=F32)


def _norm_modulate(x, g, shift, scale):
    ms = jnp.mean(x * x, axis=-1, keepdims=True)
    return x * lax.rsqrt(ms + EPS) * (g * (1.0 + scale)) + shift


def _ada_kernel(c_ref, w_ref, b_ref, o_ref):
    c = c_ref[...]
    c_act = (c * jax.nn.sigmoid(c)).astype(BF16)
    o_ref[...] = _dot(c_act, w_ref[...].astype(BF16)) + b_ref[...]


def _ada_mod(c, w_ada, b_ada, *, tn):
    bsz, d = c.shape
    n = w_ada.shape[1]
    return pl.pallas_call(
        _ada_kernel,
        grid=(n // tn,),
        in_specs=[
            pl.BlockSpec((bsz, d), lambda j: (0, 0)),
            pl.BlockSpec((d, tn), lambda j: (0, j)),
            pl.BlockSpec((1, tn), lambda j: (0, j)),
        ],
        out_specs=pl.BlockSpec((bsz, tn), lambda j: (0, j)),
        out_shape=jax.ShapeDtypeStruct((bsz, n), F32),
        compiler_params=pltpu.CompilerParams(
            dimension_semantics=("arbitrary",), vmem_limit_bytes=V7X_VMEM_LIMIT_BYTES),
        name="ada_mod",
    )(c, w_ada, b_ada.reshape(1, n))


def _ffn_kernel(x_ref, mod_ref, g_ref, w1_ref, w3_ref, w2_ref, o_ref, h_ref, acc_ref, *, mod_row):
    f = pl.program_id(1)

    @pl.when(f == 0)
    def _():
        h = _norm_modulate(x_ref[...], g_ref[...], mod_ref[0, mod_row:mod_row + 1, :],
                           mod_ref[0, mod_row + 1:mod_row + 2, :])
        h_ref[...] = h.astype(BF16)
        acc_ref[...] = jnp.zeros_like(acc_ref)

    h = h_ref[...]
    a = _dot(h, w1_ref[...])
    b = _dot(h, w3_ref[...])
    p = (a * jax.nn.sigmoid(a) * b).astype(BF16)
    acc_ref[...] += _dot(p, w2_ref[...])

    @pl.when(f == pl.num_programs(1) - 1)
    def _():
        half_gate = 0.5 * mod_ref[0, mod_row + 2:mod_row + 3, :]
        o_ref[...] = x_ref[...] + half_gate * acc_ref[...]


def _ffn(x2, mod3, g, w1, w3, w2, *, mod_row, seq, tm, tf):
    n, d = x2.shape
    dff = w1.shape[1]
    tiles_per_seq = seq // tm
    return pl.pallas_call(
        functools.partial(_ffn_kernel, mod_row=mod_row),
        grid=(n // tm, dff // tf),
        in_specs=[
            pl.BlockSpec((tm, d), lambda i, f: (i, 0)),
            pl.BlockSpec((1, N_MOD, d), lambda i, f: (i // tiles_per_seq, 0, 0)),
            pl.BlockSpec((1, d), lambda i, f: (0, 0)),
            pl.BlockSpec((d, tf), lambda i, f: (0, f)),
            pl.BlockSpec((d, tf), lambda i, f: (0, f)),
            pl.BlockSpec((tf, d), lambda i, f: (f, 0)),
        ],
        out_specs=pl.BlockSpec((tm, d), lambda i, f: (i, 0)),
        out_shape=jax.ShapeDtypeStruct((n, d), F32),
        scratch_shapes=[pltpu.VMEM((tm, d), BF16), pltpu.VMEM((tm, d), F32)],
        compiler_params=pltpu.CompilerParams(
            dimension_semantics=("arbitrary", "arbitrary"),
            vmem_limit_bytes=V7X_VMEM_LIMIT_BYTES),
        name=f"ffn_mod{mod_row}",
    )(x2, mod3, g.reshape(1, d), w1, w3, w2)


def _t5_bucket_table():
    kj = np.arange(2 * BLOCK)[:, None]
    qi = np.arange(BLOCK)[None, :]
    dist = qi + BLOCK - kj
    in_window = (dist >= 0) & (dist < BLOCK)
    n = np.clip(dist, 0, None)
    max_exact = N_BUCKETS // 2
    nf = np.maximum(n, 1).astype(np.float32)
    large = max_exact + (np.log(nf / np.float32(max_exact)) / np.float32(math.log(MAX_DISTANCE / max_exact))
                         * np.float32(N_BUCKETS - max_exact)).astype(np.int32)
    large = np.minimum(large, N_BUCKETS - 1)
    bucket = np.where(n < max_exact, n, large)
    return np.where(in_window, bucket, -1).astype(np.int32)


def _bias_kernel(rb_ref, bucket_ref, o_ref):
    bucket = bucket_ref[...]
    prev_rows = lax.broadcasted_iota(jnp.int32, bucket.shape, 0) < BLOCK
    for head in range(B_HEADS):
        acc = jnp.full(bucket.shape, MASKED, F32)
        for b in range(N_BUCKETS):
            acc = jnp.where(bucket == b, rb_ref[b, head], acc)
        j, hl = divmod(head, GROUP)
        o_ref[0, j, :, hl * BLOCK:(hl + 1) * BLOCK] = acc
        o_ref[1, j, :, hl * BLOCK:(hl + 1) * BLOCK] = jnp.where(prev_rows, MASKED, acc)


def _bias_table(rel_bias):
    return pl.pallas_call(
        _bias_kernel,
        in_specs=[
            pl.BlockSpec(memory_space=pltpu.SMEM),
            pl.BlockSpec((2 * BLOCK, BLOCK), lambda: (0, 0)),
        ],
        out_specs=pl.BlockSpec((2, B_KV_HEADS, 2 * BLOCK, GROUP * BLOCK), lambda: (0, 0, 0, 0)),
        out_shape=jax.ShapeDtypeStruct((2, B_KV_HEADS, 2 * BLOCK, GROUP * BLOCK), F32),
        name="rel_bias_table",
    )(rel_bias, jnp.asarray(_t5_bucket_table()))


def _gelu(x):
    return 0.5 * x * (1.0 + lax.erf(x * (1.0 / math.sqrt(2.0))))


def _rms_rows(xt, gain):
    ms = jnp.mean(xt * xt, axis=0, keepdims=True)
    return xt * lax.rsqrt(ms + EPS) * gain


def _mix_kernel(sinks_ref, x_ref, mod_ref, g_ref, wuv_ref, wqkvt_ref, wout_ref, sw_ref, sb_ref,
                gv_ref, gq_ref, gk_ref, bias_ref, o_ref, kprev_ref, vprev_ref, *, tiles_per_seq):
    tm = x_ref.shape[0]
    nblk = tm // BLOCK
    first = pl.program_id(0) % tiles_per_seq == 0

    @pl.when(first)
    def _():
        kprev_ref[...] = jnp.zeros_like(kprev_ref)
        vprev_ref[...] = jnp.zeros_like(vprev_ref)

    x = x_ref[...]
    h = _norm_modulate(x, g_ref[...], mod_ref[0, 3:4, :], mod_ref[0, 4:5, :]).astype(BF16)

    zuv = _dot(h, wuv_ref[...])
    u = _gelu(zuv[:, :A_WIDTH])
    v = _gelu(zuv[:, A_WIDTH:])
    row = lax.broadcasted_iota(jnp.int32, (BLOCK, BLOCK), 0)
    col = lax.broadcasted_iota(jnp.int32, (BLOCK, BLOCK), 1)
    causal = row >= col
    ya_heads = []
    for hd in range(A_HEADS):
        cols = slice(hd * A_HEAD_DIM, (hd + 1) * A_HEAD_DIM)
        vh = v[:, cols]
        ms = jnp.mean(vh * vh, axis=-1, keepdims=True)
        vn = (vh * lax.rsqrt(ms + EPS) * gv_ref[:, cols]).astype(BF16)
        w_tril = jnp.where(causal, sw_ref[hd], 0.0).astype(BF16)
        mixed = [_dot(w_tril, vn[c * BLOCK:(c + 1) * BLOCK]) + sb_ref[:, cols] for c in range(nblk)]
        ya_heads.append(u[:, cols] * jnp.concatenate(mixed, axis=0))
    ya = jnp.concatenate(ya_heads, axis=1)

    zt = _dot_nt(wqkvt_ref[...], h)
    gq = gq_ref[...] * (HEAD_DIM ** -0.5)
    qn = [_rms_rows(zt[hd * HEAD_DIM:(hd + 1) * HEAD_DIM], jnp.tile(gq, (1, nblk))).astype(BF16)
          for hd in range(B_HEADS)]
    kn, vv = [], []
    for j in range(B_KV_HEADS):
        k_rows = slice(B_WIDTH + j * HEAD_DIM, B_WIDTH + (j + 1) * HEAD_DIM)
        v_rows = slice(B_WIDTH + KV_WIDTH + j * HEAD_DIM, B_WIDTH + KV_WIDTH + (j + 1) * HEAD_DIM)
        kn.append(_rms_rows(zt[k_rows], jnp.tile(gk_ref[...], (1, nblk))).astype(BF16))
        vv.append(zt[v_rows].astype(BF16))

    variant = jnp.where(first, 1, 0)
    yb_blocks = []
    for c in range(nblk):
        tok = slice(c * BLOCK, (c + 1) * BLOCK)
        head_rows = [None] * B_HEADS
        for j in range(B_KV_HEADS):
            if c == 0:
                k_prev, v_prev = kprev_ref[j], vprev_ref[j]
                bias = bias_ref[variant, j]
            else:
                prev = slice((c - 1) * BLOCK, c * BLOCK)
                k_prev, v_prev = kn[j][:, prev], vv[j][:, prev]
                bias = bias_ref[0, j]
            k_band = jnp.concatenate([k_prev, kn[j][:, tok]], axis=1)
            v_band = jnp.concatenate([v_prev, vv[j][:, tok]], axis=1)
            q_grp = jnp.concatenate([qn[j * GROUP + hl][:, tok] for hl in range(GROUP)], axis=1)
            s = _dot_tn(k_band, q_grp) + bias
            sink = jnp.concatenate(
                [jnp.full((1, BLOCK), sinks_ref[j * GROUP + hl], F32) for hl in range(GROUP)], axis=1)
            m = jnp.maximum(jnp.max(s, axis=0, keepdims=True), sink)
            p = jnp.exp(s - m)
            denom = jnp.sum(p, axis=0, keepdims=True) + jnp.exp(sink - m)
            o_t = _dot(v_band, p.astype(BF16)) / denom
            for hl in range(GROUP):
                head_rows[j * GROUP + hl] = o_t[:, hl * BLOCK:(hl + 1) * BLOCK]
        yb_blocks.append(jnp.concatenate(head_rows, axis=0))
    yb = jnp.concatenate(yb_blocks, axis=1).T

    last = slice((nblk - 1) * BLOCK, nblk * BLOCK)
    for j in range(B_KV_HEADS):
        kprev_ref[j] = kn[j][:, last]
        vprev_ref[j] = vv[j][:, last]

    y = _dot(jnp.concatenate([ya, yb], axis=1).astype(BF16), wout_ref[...])
    o_ref[...] = x + mod_ref[0, 5:6, :] * y


def _mix(x2, mod3, g_mix, w_uv, w_qkvt, w_out, spatial_w, sb_full, gv_row, gq_tab, gk_tab, sinks,
         bias_tab, *, seq, tm):
    n, d = x2.shape
    tiles_per_seq = seq // tm
    const2 = lambda i: (0, 0)
    return pl.pallas_call(
        functools.partial(_mix_kernel, tiles_per_seq=tiles_per_seq),
        grid=(n // tm,),
        in_specs=[
            pl.BlockSpec(memory_space=pltpu.SMEM),
            pl.BlockSpec((tm, d), lambda i: (i, 0)),
            pl.BlockSpec((1, N_MOD, d), lambda i: (i // tiles_per_seq, 0, 0)),
            pl.BlockSpec((1, d), const2),
            pl.BlockSpec(w_uv.shape, const2, pipeline_mode=pl.Buffered(1)),
            pl.BlockSpec(w_qkvt.shape, const2, pipeline_mode=pl.Buffered(1)),
            pl.BlockSpec(w_out.shape, const2, pipeline_mode=pl.Buffered(1)),
            pl.BlockSpec(spatial_w.shape, lambda i: (0, 0, 0), pipeline_mode=pl.Buffered(1)),
            pl.BlockSpec(sb_full.shape, const2, pipeline_mode=pl.Buffered(1)),
            pl.BlockSpec(gv_row.shape, const2),
            pl.BlockSpec(gq_tab.shape, const2),
            pl.BlockSpec(gk_tab.shape, const2),
            pl.BlockSpec(bias_tab.shape, lambda i: (0, 0, 0, 0), pipeline_mode=pl.Buffered(1)),
        ],
        out_specs=pl.BlockSpec((tm, d), lambda i: (i, 0)),
        scratch_shapes=[pltpu.VMEM((B_KV_HEADS, HEAD_DIM, BLOCK), BF16),
                        pltpu.VMEM((B_KV_HEADS, HEAD_DIM, BLOCK), BF16)],
        out_shape=jax.ShapeDtypeStruct((n, d), F32),
        compiler_params=pltpu.CompilerParams(
            dimension_semantics=("arbitrary",), vmem_limit_bytes=V7X_VMEM_LIMIT_BYTES),
        name="token_mix",
    )(sinks, x2, mod3, g_mix.reshape(1, d), w_uv, w_qkvt, w_out, spatial_w, sb_full, gv_row,
      gq_tab, gk_tab, bias_tab)


def _tile_sizes(seq, dff):
    tm_ffn = next(t for t in (512, 256, 128) if seq % t == 0)
    tf = next(t for t in (512, 256, 128) if dff % t == 0)
    tm_mix = next(t for t in (256, 128) if seq % t == 0)
    return tm_ffn, tf, tm_mix


def kernel(x, c, w_ada, b_ada, g_ffn1, w1_ffn1, w3_ffn1, w2_ffn1, g_mix, w_in, spatial_w, spatial_b,
           g_v, g_q, g_k, sinks, rel_bias, w_out, g_ffn2, w1_ffn2, w3_ffn2, w2_ffn2):
    bsz, seq, d = x.shape
    depth = w_ada.shape[0]
    dff = w1_ffn1.shape[-1]
    assert seq % BLOCK == 0 and w_in.shape[-1] == 2 * A_WIDTH + QKV_WIDTH
    tm_ffn, tf, tm_mix = _tile_sizes(seq, dff)
    ada_tn = next(t for t in (1024, 512, 256, 128) if (N_MOD * d) % t == 0)

    bias_tab = _bias_table(rel_bias)
    x2 = x.reshape(bsz * seq, d)
    for l in range(depth):
        mod3 = _ada_mod(c, w_ada[l], b_ada[l], tn=ada_tn).reshape(bsz, N_MOD, d)
        x2 = _ffn(x2, mod3, g_ffn1[l], w1_ffn1[l].astype(BF16), w3_ffn1[l].astype(BF16),
                  w2_ffn1[l].astype(BF16), mod_row=0, seq=seq, tm=tm_ffn, tf=tf)
        w_in_l = w_in[l]
        x2 = _mix(
            x2, mod3, g_mix[l],
            w_in_l[:, :2 * A_WIDTH].astype(BF16),
            w_in_l[:, 2 * A_WIDTH:].T.astype(BF16),
            w_out[l].astype(BF16),
            spatial_w[l],
            jnp.repeat(spatial_b[l].T, A_HEAD_DIM, axis=1),
            g_v[l].reshape(1, A_WIDTH),
            jnp.broadcast_to(g_q[l][:, None], (HEAD_DIM, BLOCK)),
            jnp.broadcast_to(g_k[l][:, None], (HEAD_DIM, BLOCK)),
            sinks[l], bias_tab, seq=seq, tm=tm_mix)
        x2 = _ffn(x2, mod3, g_ffn2[l], w1_ffn2[l].astype(BF16), w3_ffn2[l].astype(BF16),
                  w2_ffn2[l].astype(BF16), mod_row=6, seq=seq, tm=tm_ffn, tf=tf)
    return x2.reshape(bsz, seq, d)
```

```python
import functools
import math

import jax
import jax.numpy as jnp
import numpy as np
from jax import lax
from jax.experimental import pallas as pl
from jax.experimental.pallas import tpu as pltpu

BLOCK = 128
A_HEADS = 8
A_HEAD_DIM = 128
A_WIDTH = A_HEADS * A_HEAD_DIM
B_HEADS = 16
B_KV_HEADS = 2
GROUP = B_HEADS // B_KV_HEADS
HEAD_DIM = 64
B_WIDTH = B_HEADS * HEAD_DIM
KV_WIDTH = B_KV_HEADS * HEAD_DIM
QKV_WIDTH = B_WIDTH + 2 * KV_WIDTH
N_BUCKETS = 32
MAX_DISTANCE = 128
N_MOD = 9
EPS = 1e-6
MASKED = -1e30

V7X_VMEM_LIMIT_BYTES = 60000 * 1024

F32 = jnp.float32
BF16 = jnp.bfloat16


def _dot(a, b):
    return jnp.dot(a, b, preferred_element_type=F32)


def _dot_nt(a, b):
    return lax.dot_general(a, b, (((1,), (1,)), ((), ())), preferred_element_type=F32)


def _dot_tn(a, b):
    return lax.dot_general(a, b, (((0,), (0,)), ((), ())), preferred_element_type=F32)


NORM_ROWS = 16


def _norm_modulate_into(h_ref, x_ref, g, shift, scale, copy_ref=None):
    gain = g * (1.0 + scale)
    for r0 in range(0, x_ref.shape[0], NORM_ROWS):
        rows = slice(r0, r0 + NORM_ROWS)
        x = x_ref[rows, :]
        ms = jnp.mean(x * x, axis=-1, keepdims=True)
        h_ref[rows, :] = (x * lax.rsqrt(ms + EPS) * gain + shift).astype(BF16)
        if copy_ref is not None:
            copy_ref[rows, :] = x


def _ada_kernel(c_ref, w_ref, b_ref, o_ref):
    c = c_ref[...]
    c_act = (c * jax.nn.sigmoid(c)).astype(BF16)
    o_ref[...] = _dot(c_act, w_ref[...].astype(BF16)) + b_ref[...]


def _ada_mod(c, w_ada, b_ada, *, tn):
    bsz, d = c.shape
    n = w_ada.shape[1]
    return pl.pallas_call(
        _ada_kernel,
        grid=(n // tn,),
        in_specs=[
            pl.BlockSpec((bsz, d), lambda j: (0, 0)),
            pl.BlockSpec((d, tn), lambda j: (0, j)),
            pl.BlockSpec((1, tn), lambda j: (0, j)),
        ],
        out_specs=pl.BlockSpec((bsz, tn), lambda j: (0, j)),
        out_shape=jax.ShapeDtypeStruct((bsz, n), F32),
        compiler_params=pltpu.CompilerParams(
            dimension_semantics=("arbitrary",), vmem_limit_bytes=V7X_VMEM_LIMIT_BYTES),
        name="ada_mod",
    )(c, w_ada, b_ada.reshape(1, n))


def _ffn_kernel(x_ref, mod_ref, g_ref, w1_ref, w3_ref, w2_ref, o_ref, h_ref, *, mod_row):
    f = pl.program_id(1)

    @pl.when(f == 0)
    def _():
        _norm_modulate_into(h_ref, x_ref, g_ref[...], mod_ref[0, mod_row:mod_row + 1, :],
                            mod_ref[0, mod_row + 1:mod_row + 2, :], copy_ref=o_ref)

    h = h_ref[...]
    a = _dot(h, w1_ref[...])
    b = _dot(h, w3_ref[...])
    p = (a * jax.nn.sigmoid(a) * b).astype(BF16)
    o_ref[...] += (0.5 * mod_ref[0, mod_row + 2:mod_row + 3, :]) * _dot(p, w2_ref[...])


def _ffn(x2, mod3, g, w1, w3, w2, *, mod_row, seq, tm, tf):
    n, d = x2.shape
    dff = w1.shape[1]
    tiles_per_seq = seq // tm
    return pl.pallas_call(
        functools.partial(_ffn_kernel, mod_row=mod_row),
        grid=(n // tm, dff // tf),
        in_specs=[
            pl.BlockSpec((tm, d), lambda i, f: (i, 0)),
            pl.BlockSpec((1, N_MOD, d), lambda i, f: (i // tiles_per_seq, 0, 0)),
            pl.BlockSpec((1, d), lambda i, f: (0, 0)),
            pl.BlockSpec((d, tf), lambda i, f: (0, f)),
            pl.BlockSpec((d, tf), lambda i, f: (0, f)),
            pl.BlockSpec((tf, d), lambda i, f: (f, 0)),
        ],
        out_specs=pl.BlockSpec((tm, d), lambda i, f: (i, 0)),
        out_shape=jax.ShapeDtypeStruct((n, d), F32),
        scratch_shapes=[pltpu.VMEM((tm, d), BF16)],
        compiler_params=pltpu.CompilerParams(
            dimension_semantics=("arbitrary", "arbitrary"),
            vmem_limit_bytes=V7X_VMEM_LIMIT_BYTES),
        name=f"ffn_mod{mod_row}",
    )(x2, mod3, g.reshape(1, d), w1, w3, w2)


def _t5_bucket_table():
    kj = np.arange(2 * BLOCK)[:, None]
    qi = np.arange(BLOCK)[None, :]
    dist = qi + BLOCK - kj
    in_window = (dist >= 0) & (dist < BLOCK)
    n = np.clip(dist, 0, None)
    max_exact = N_BUCKETS // 2
    nf = np.maximum(n, 1).astype(np.float32)
    large = max_exact + (np.log(nf / np.float32(max_exact)) / np.float32(math.log(MAX_DISTANCE / max_exact))
                         * np.float32(N_BUCKETS - max_exact)).astype(np.int32)
    large = np.minimum(large, N_BUCKETS - 1)
    bucket = np.where(n < max_exact, n, large)
    return np.where(in_window, bucket, -1).astype(np.int32)


def _bias_kernel(rb_ref, bucket_ref, o_ref):
    bucket = bucket_ref[...]
    prev_rows = lax.broadcasted_iota(jnp.int32, bucket.shape, 0) < BLOCK
    for head in range(B_HEADS):
        acc = jnp.full(bucket.shape, MASKED, F32)
        for b in range(N_BUCKETS):
            acc = jnp.where(bucket == b, rb_ref[b, head], acc)
        j, hl = divmod(head, GROUP)
        o_ref[0, j, :, hl * BLOCK:(hl + 1) * BLOCK] = acc
        o_ref[1, j, :, hl * BLOCK:(hl + 1) * BLOCK] = jnp.where(prev_rows, MASKED, acc)


def _bias_table(rel_bias):
    return pl.pallas_call(
        _bias_kernel,
        in_specs=[
            pl.BlockSpec(memory_space=pltpu.SMEM),
            pl.BlockSpec((2 * BLOCK, BLOCK), lambda: (0, 0)),
        ],
        out_specs=pl.BlockSpec((2, B_KV_HEADS, 2 * BLOCK, GROUP * BLOCK), lambda: (0, 0, 0, 0)),
        out_shape=jax.ShapeDtypeStruct((2, B_KV_HEADS, 2 * BLOCK, GROUP * BLOCK), F32),
        name="rel_bias_table",
    )(rel_bias, jnp.asarray(_t5_bucket_table()))


def _gelu(x):
    return 0.5 * x * (1.0 + lax.erf(x * (1.0 / math.sqrt(2.0))))


def _rms_rows(xt, gain):
    ms = jnp.mean(xt * xt, axis=0, keepdims=True)
    return xt * lax.rsqrt(ms + EPS) * gain


def _mix_kernel(sinks_ref, x_ref, mod_ref, g_ref, wuv_ref, wqkvt_ref, wout_ref, sw_ref, sb_ref,
                gv_ref, gq_ref, gk_ref, bias_ref, o_ref, h_ref, kprev_ref, vprev_ref, *, tiles_per_seq):
    tm = x_ref.shape[0]
    nblk = tm // BLOCK
    first = pl.program_id(0) % tiles_per_seq == 0

    @pl.when(first)
    def _():
        kprev_ref[...] = jnp.zeros_like(kprev_ref)
        vprev_ref[...] = jnp.zeros_like(vprev_ref)

    _norm_modulate_into(h_ref, x_ref, g_ref[...], mod_ref[0, 3:4, :], mod_ref[0, 4:5, :])
    h = h_ref[...]

    zuv = _dot(h, wuv_ref[...])
    u = _gelu(zuv[:, :A_WIDTH])
    v = _gelu(zuv[:, A_WIDTH:])
    row = lax.broadcasted_iota(jnp.int32, (BLOCK, BLOCK), 0)
    col = lax.broadcasted_iota(jnp.int32, (BLOCK, BLOCK), 1)
    causal = row >= col
    ya_heads = []
    for hd in range(A_HEADS):
        cols = slice(hd * A_HEAD_DIM, (hd + 1) * A_HEAD_DIM)
        vh = v[:, cols]
        ms = jnp.mean(vh * vh, axis=-1, keepdims=True)
        vn = (vh * lax.rsqrt(ms + EPS) * gv_ref[:, cols]).astype(BF16)
        w_tril = jnp.where(causal, sw_ref[hd], 0.0).astype(BF16)
        mixed = [_dot(w_tril, vn[c * BLOCK:(c + 1) * BLOCK]) + sb_ref[:, cols] for c in range(nblk)]
        ya_heads.append(u[:, cols] * jnp.concatenate(mixed, axis=0))
    ya = jnp.concatenate(ya_heads, axis=1)

    zt = _dot_nt(wqkvt_ref[...], h)
    gq = gq_ref[...] * (HEAD_DIM ** -0.5)
    qn = [_rms_rows(zt[hd * HEAD_DIM:(hd + 1) * HEAD_DIM], jnp.tile(gq, (1, nblk))).astype(BF16)
          for hd in range(B_HEADS)]
    kn, vv = [], []
    for j in range(B_KV_HEADS):
        k_rows = slice(B_WIDTH + j * HEAD_DIM, B_WIDTH + (j + 1) * HEAD_DIM)
        v_rows = slice(B_WIDTH + KV_WIDTH + j * HEAD_DIM, B_WIDTH + KV_WIDTH + (j + 1) * HEAD_DIM)
        kn.append(_rms_rows(zt[k_rows], jnp.tile(gk_ref[...], (1, nblk))).astype(BF16))
        vv.append(zt[v_rows].astype(BF16))

    variant = jnp.where(first, 1, 0)
    yb_blocks = []
    for c in range(nblk):
        tok = slice(c * BLOCK, (c + 1) * BLOCK)
        head_rows = [None] * B_HEADS
        for j in range(B_KV_HEADS):
            if c == 0:
                k_prev, v_prev = kprev_ref[j], vprev_ref[j]
                bias = bias_ref[variant, j]
            else:
                prev = slice((c - 1) * BLOCK, c * BLOCK)
                k_prev, v_prev = kn[j][:, prev], vv[j][:, prev]
                bias = bias_ref[0, j]
            k_band = jnp.concatenate([k_prev, kn[j][:, tok]], axis=1)
            v_band = jnp.concatenate([v_prev, vv[j][:, tok]], axis=1)
            q_grp = jnp.concatenate([qn[j * GROUP + hl][:, tok] for hl in range(GROUP)], axis=1)
            s = _dot_tn(k_band, q_grp) + bias
            sink = jnp.concatenate(
                [jnp.full((1, BLOCK), sinks_ref[j * GROUP + hl], F32) for hl in range(GROUP)], axis=1)
            m = jnp.maximum(jnp.max(s, axis=0, keepdims=True), sink)
            p = jnp.exp(s - m)
            denom = jnp.sum(p, axis=0, keepdims=True) + jnp.exp(sink - m)
            o_t = _dot(v_band, p.astype(BF16)) / denom
            for hl in range(GROUP):
                head_rows[j * GROUP + hl] = o_t[:, hl * BLOCK:(hl + 1) * BLOCK]
        yb_blocks.append(jnp.concatenate(head_rows, axis=0))
    yb = jnp.concatenate(yb_blocks, axis=1).T

    last = slice((nblk - 1) * BLOCK, nblk * BLOCK)
    for j in range(B_KV_HEADS):
        kprev_ref[j] = kn[j][:, last]
        vprev_ref[j] = vv[j][:, last]

    y = _dot(ya.astype(BF16), wout_ref[:A_WIDTH, :]) + _dot(yb.astype(BF16), wout_ref[A_WIDTH:, :])
    o_ref[...] = x_ref[...] + mod_ref[0, 5:6, :] * y


def _mix(x2, mod3, g_mix, w_uv, w_qkvt, w_out, spatial_w, sb_full, gv_row, gq_tab, gk_tab, sinks,
         bias_tab, *, seq, tm):
    n, d = x2.shape
    tiles_per_seq = seq // tm
    const2 = lambda i: (0, 0)
    return pl.pallas_call(
        functools.partial(_mix_kernel, tiles_per_seq=tiles_per_seq),
        grid=(n // tm,),
        in_specs=[
            pl.BlockSpec(memory_space=pltpu.SMEM),
            pl.BlockSpec((tm, d), lambda i: (i, 0)),
            pl.BlockSpec((1, N_MOD, d), lambda i: (i // tiles_per_seq, 0, 0)),
            pl.BlockSpec((1, d), const2),
            pl.BlockSpec(w_uv.shape, const2, pipeline_mode=pl.Buffered(1)),
            pl.BlockSpec(w_qkvt.shape, const2, pipeline_mode=pl.Buffered(1)),
            pl.BlockSpec(w_out.shape, const2, pipeline_mode=pl.Buffered(1)),
            pl.BlockSpec(spatial_w.shape, lambda i: (0, 0, 0), pipeline_mode=pl.Buffered(1)),
            pl.BlockSpec(sb_full.shape, const2, pipeline_mode=pl.Buffered(1)),
            pl.BlockSpec(gv_row.shape, const2),
            pl.BlockSpec(gq_tab.shape, const2),
            pl.BlockSpec(gk_tab.shape, const2),
            pl.BlockSpec(bias_tab.shape, lambda i: (0, 0, 0, 0), pipeline_mode=pl.Buffered(1)),
        ],
        out_specs=pl.BlockSpec((tm, d), lambda i: (i, 0)),
        scratch_shapes=[pltpu.VMEM((tm, d), BF16),
                        pltpu.VMEM((B_KV_HEADS, HEAD_DIM, BLOCK), BF16),
                        pltpu.VMEM((B_KV_HEADS, HEAD_DIM, BLOCK), BF16)],
        out_shape=jax.ShapeDtypeStruct((n, d), F32),
        compiler_params=pltpu.CompilerParams(
            dimension_semantics=("arbitrary",), vmem_limit_bytes=V7X_VMEM_LIMIT_BYTES),
        name="token_mix",
    )(sinks, x2, mod3, g_mix.reshape(1, d), w_uv, w_qkvt, w_out, spatial_w, sb_full, gv_row,
      gq_tab, gk_tab, bias_tab)


def _tile_sizes(seq, dff):
    tm_ffn = next(t for t in (1024, 512, 256, 128) if seq % t == 0)
    tf = next(t for t in (512, 256, 128) if dff % t == 0)
    tm_mix = next(t for t in (512, 256, 128) if seq % t == 0)
    return tm_ffn, tf, tm_mix


def kernel(x, c, w_ada, b_ada, g_ffn1, w1_ffn1, w3_ffn1, w2_ffn1, g_mix, w_in, spatial_w, spatial_b,
           g_v, g_q, g_k, sinks, rel_bias, w_out, g_ffn2, w1_ffn2, w3_ffn2, w2_ffn2):
    bsz, seq, d = x.shape
    depth = w_ada.shape[0]
    dff = w1_ffn1.shape[-1]
    assert seq % BLOCK == 0 and w_in.shape[-1] == 2 * A_WIDTH + QKV_WIDTH
    tm_ffn, tf, tm_mix = _tile_sizes(seq, dff)
    ada_tn = next(t for t in (1024, 512, 256, 128) if (N_MOD * d) % t == 0)

    bias_tab = _bias_table(rel_bias)
    x2 = x.reshape(bsz * seq, d)
    for l in range(depth):
        mod3 = _ada_mod(c, w_ada[l], b_ada[l], tn=ada_tn).reshape(bsz, N_MOD, d)
        x2 = _ffn(x2, mod3, g_ffn1[l], w1_ffn1[l].astype(BF16), w3_ffn1[l].astype(BF16),
                  w2_ffn1[l].astype(BF16), mod_row=0, seq=seq, tm=tm_ffn, tf=tf)
        w_in_l = w_in[l]
        x2 = _mix(
            x2, mod3, g_mix[l],
            w_in_l[:, :2 * A_WIDTH].astype(BF16),
            w_in_l[:, 2 * A_WIDTH:].T.astype(BF16),
            w_out[l].astype(BF16),
            spatial_w[l],
            jnp.repeat(spatial_b[l].T, A_HEAD_DIM, axis=1),
            g_v[l].reshape(1, A_WIDTH),
            jnp.broadcast_to(g_q[l][:, None], (HEAD_DIM, BLOCK)),
            jnp.broadcast_to(g_k[l][:, None], (HEAD_DIM, BLOCK)),
            sinks[l], bias_tab, seq=seq, tm=tm_mix)
        x2 = _ffn(x2, mod3, g_ffn2[l], w1_ffn2[l].astype(BF16), w3_ffn2[l].astype(BF16),
                  w2_ffn2[l].astype(BF16), mod_row=6, seq=seq, tm=tm_ffn, tf=tf)
    return x2.reshape(bsz, seq, d)
```

```python
import functools
import math

import jax
import jax.numpy as jnp
import numpy as np
from jax import lax
from jax.experimental import pallas as pl
from jax.experimental.pallas import tpu as pltpu

BLOCK = 128
A_HEADS = 8
A_HEAD_DIM = 128
A_WIDTH = A_HEADS * A_HEAD_DIM
B_HEADS = 16
B_KV_HEADS = 2
GROUP = B_HEADS // B_KV_HEADS
HEAD_DIM = 64
B_WIDTH = B_HEADS * HEAD_DIM
KV_WIDTH = B_KV_HEADS * HEAD_DIM
QKV_WIDTH = B_WIDTH + 2 * KV_WIDTH
N_BUCKETS = 32
MAX_DISTANCE = 128
N_MOD = 9
EPS = 1e-6
MASKED = -1e30

V7X_VMEM_LIMIT_BYTES = 60000 * 1024

F32 = jnp.float32
BF16 = jnp.bfloat16


def _dot(a, b):
    return jnp.dot(a, b, preferred_element_type=F32)


def _dot_nt(a, b):
    return lax.dot_general(a, b, (((1,), (1,)), ((), ())), preferred_element_type=F32)


def _dot_tn(a, b):
    return lax.dot_general(a, b, (((0,), (0,)), ((), ())), preferred_element_type=F32)


NORM_ROWS = 16


def _norm_modulate_into(h_ref, x_ref, g, shift, scale):
    gain = g * (1.0 + scale)
    for r0 in range(0, x_ref.shape[0], NORM_ROWS):
        rows = slice(r0, r0 + NORM_ROWS)
        x = x_ref[rows, :]
        ms = jnp.mean(x * x, axis=-1, keepdims=True)
        h_ref[rows, :] = (x * lax.rsqrt(ms + EPS) * gain + shift).astype(BF16)


def _ada_kernel(c_ref, w_ref, b_ref, o_ref):
    c = c_ref[...]
    c_act = (c * jax.nn.sigmoid(c)).astype(BF16)
    o_ref[...] = _dot(c_act, w_ref[...].astype(BF16)) + b_ref[...]


def _ada_mod(c, w_ada, b_ada, *, tn):
    bsz, d = c.shape
    n = w_ada.shape[1]
    return pl.pallas_call(
        _ada_kernel,
        grid=(n // tn,),
        in_specs=[
            pl.BlockSpec((bsz, d), lambda j: (0, 0)),
            pl.BlockSpec((d, tn), lambda j: (0, j)),
            pl.BlockSpec((1, tn), lambda j: (0, j)),
        ],
        out_specs=pl.BlockSpec((bsz, tn), lambda j: (0, j)),
        out_shape=jax.ShapeDtypeStruct((bsz, n), F32),
        compiler_params=pltpu.CompilerParams(
            dimension_semantics=("arbitrary",), vmem_limit_bytes=V7X_VMEM_LIMIT_BYTES),
        name="ada_mod",
    )(c, w_ada, b_ada.reshape(1, n))


def _ffn_kernel(x_ref, mod_ref, g_ref, w1_ref, w3_ref, w2_ref, o_ref, h_ref, *, mod_row):
    f = pl.program_id(1)

    def chunk(first):
        if first:
            _norm_modulate_into(h_ref, x_ref, g_ref[...], mod_ref[0, mod_row:mod_row + 1, :],
                                mod_ref[0, mod_row + 1:mod_row + 2, :])
        h = h_ref[...]
        a = _dot(h, w1_ref[...])
        b = _dot(h, w3_ref[...])
        p = (a * jax.nn.sigmoid(a) * b).astype(BF16)
        y = (0.5 * mod_ref[0, mod_row + 2:mod_row + 3, :]) * _dot(p, w2_ref[...])
        if first:
            o_ref[...] = x_ref[...] + y
        else:
            o_ref[...] += y

    @pl.when(f == 0)
    def _():
        chunk(True)

    @pl.when(f != 0)
    def _():
        chunk(False)


def _ffn(x2, mod3, g, w1, w3, w2, *, mod_row, seq, tm, tf):
    n, d = x2.shape
    dff = w1.shape[1]
    tiles_per_seq = seq // tm
    return pl.pallas_call(
        functools.partial(_ffn_kernel, mod_row=mod_row),
        grid=(n // tm, dff // tf),
        in_specs=[
            pl.BlockSpec((tm, d), lambda i, f: (i, 0)),
            pl.BlockSpec((1, N_MOD, d), lambda i, f: (i // tiles_per_seq, 0, 0)),
            pl.BlockSpec((1, d), lambda i, f: (0, 0)),
            pl.BlockSpec((d, tf), lambda i, f: (0, f)),
            pl.BlockSpec((d, tf), lambda i, f: (0, f)),
            pl.BlockSpec((tf, d), lambda i, f: (f, 0)),
        ],
        out_specs=pl.BlockSpec((tm, d), lambda i, f: (i, 0)),
        out_shape=jax.ShapeDtypeStruct((n, d), F32),
        scratch_shapes=[pltpu.VMEM((tm, d), BF16)],
        compiler_params=pltpu.CompilerParams(
            dimension_semantics=("arbitrary", "arbitrary"),
            vmem_limit_bytes=V7X_VMEM_LIMIT_BYTES),
        name=f"ffn_mod{mod_row}",
    )(x2, mod3, g.reshape(1, d), w1, w3, w2)


def _t5_bucket_table():
    kj = np.arange(2 * BLOCK)[:, None]
    qi = np.arange(BLOCK)[None, :]
    dist = qi + BLOCK - kj
    in_window = (dist >= 0) & (dist < BLOCK)
    n = np.clip(dist, 0, None)
    max_exact = N_BUCKETS // 2
    nf = np.maximum(n, 1).astype(np.float32)
    large = max_exact + (np.log(nf / np.float32(max_exact)) / np.float32(math.log(MAX_DISTANCE / max_exact))
                         * np.float32(N_BUCKETS - max_exact)).astype(np.int32)
    large = np.minimum(large, N_BUCKETS - 1)
    bucket = np.where(n < max_exact, n, large)
    return np.where(in_window, bucket, -1).astype(np.int32)


def _bias_kernel(rb_ref, bucket_ref, o_ref):
    bucket = bucket_ref[...]
    prev_rows = lax.broadcasted_iota(jnp.int32, bucket.shape, 0) < BLOCK
    for head in range(B_HEADS):
        acc = jnp.full(bucket.shape, MASKED, F32)
        for b in range(N_BUCKETS):
            acc = jnp.where(bucket == b, rb_ref[b, head], acc)
        j, hl = divmod(head, GROUP)
        o_ref[0, j, :, hl * BLOCK:(hl + 1) * BLOCK] = acc
        o_ref[1, j, :, hl * BLOCK:(hl + 1) * BLOCK] = jnp.where(prev_rows, MASKED, acc)


def _bias_table(rel_bias):
    return pl.pallas_call(
        _bias_kernel,
        in_specs=[
            pl.BlockSpec(memory_space=pltpu.SMEM),
            pl.BlockSpec((2 * BLOCK, BLOCK), lambda: (0, 0)),
        ],
        out_specs=pl.BlockSpec((2, B_KV_HEADS, 2 * BLOCK, GROUP * BLOCK), lambda: (0, 0, 0, 0)),
        out_shape=jax.ShapeDtypeStruct((2, B_KV_HEADS, 2 * BLOCK, GROUP * BLOCK), F32),
        name="rel_bias_table",
    )(rel_bias, jnp.asarray(_t5_bucket_table()))


def _gelu(x):
    return 0.5 * x * (1.0 + lax.erf(x * (1.0 / math.sqrt(2.0))))


def _rms_rows(xt, gain):
    ms = jnp.mean(xt * xt, axis=0, keepdims=True)
    return xt * lax.rsqrt(ms + EPS) * gain


def _mix_kernel(sinks_ref, x_ref, mod_ref, g_ref, wuv_ref, wqkvt_ref, wout_ref, sw_ref, sb_ref,
                gv_ref, gq_ref, gk_ref, bias_ref, o_ref, h_ref, kprev_ref, vprev_ref, *, tiles_per_seq):
    tm = x_ref.shape[0]
    nblk = tm // BLOCK
    first = pl.program_id(0) % tiles_per_seq == 0

    @pl.when(first)
    def _():
        kprev_ref[...] = jnp.zeros_like(kprev_ref)
        vprev_ref[...] = jnp.zeros_like(vprev_ref)

    _norm_modulate_into(h_ref, x_ref, g_ref[...], mod_ref[0, 3:4, :], mod_ref[0, 4:5, :])
    h = h_ref[...]

    zuv = _dot(h, wuv_ref[...])
    u = _gelu(zuv[:, :A_WIDTH])
    v = _gelu(zuv[:, A_WIDTH:])
    row = lax.broadcasted_iota(jnp.int32, (BLOCK, BLOCK), 0)
    col = lax.broadcasted_iota(jnp.int32, (BLOCK, BLOCK), 1)
    causal = row >= col
    ya_heads = []
    for hd in range(A_HEADS):
        cols = slice(hd * A_HEAD_DIM, (hd + 1) * A_HEAD_DIM)
        vh = v[:, cols]
        ms = jnp.mean(vh * vh, axis=-1, keepdims=True)
        vn = (vh * lax.rsqrt(ms + EPS) * gv_ref[:, cols]).astype(BF16)
        w_tril = jnp.where(causal, sw_ref[hd], 0.0).astype(BF16)
        mixed = [_dot(w_tril, vn[c * BLOCK:(c + 1) * BLOCK]) + sb_ref[:, cols] for c in range(nblk)]
        ya_heads.append(u[:, cols] * jnp.concatenate(mixed, axis=0))
    ya = jnp.concatenate(ya_heads, axis=1).astype(BF16)

    gate = mod_ref[0, 5:6, :]
    d_model = o_ref.shape[1]
    slab = max(d_model // (nblk * B_KV_HEADS), BLOCK)
    n_slabs = d_model // slab

    def out_proj_a(unit):
        if unit < n_slabs:
            cols = slice(unit * slab, (unit + 1) * slab)
            o_ref[:, cols] = x_ref[:, cols] + gate[:, cols] * _dot(ya, wout_ref[:A_WIDTH, cols])

    zt = _dot_nt(wqkvt_ref[...], h)
    gq = gq_ref[...] * (HEAD_DIM ** -0.5)
    qn = [_rms_rows(zt[hd * HEAD_DIM:(hd + 1) * HEAD_DIM], jnp.tile(gq, (1, nblk))).astype(BF16)
          for hd in range(B_HEADS)]
    kn, vv = [], []
    for j in range(B_KV_HEADS):
        k_rows = slice(B_WIDTH + j * HEAD_DIM, B_WIDTH + (j + 1) * HEAD_DIM)
        v_rows = slice(B_WIDTH + KV_WIDTH + j * HEAD_DIM, B_WIDTH + KV_WIDTH + (j + 1) * HEAD_DIM)
        kn.append(_rms_rows(zt[k_rows], jnp.tile(gk_ref[...], (1, nblk))).astype(BF16))
        vv.append(zt[v_rows].astype(BF16))

    variant = jnp.where(first, 1, 0)
    yb_blocks = []
    for c in range(nblk):
        tok = slice(c * BLOCK, (c + 1) * BLOCK)
        head_rows = [None] * B_HEADS
        for j in range(B_KV_HEADS):
            if c == 0:
                k_prev, v_prev = kprev_ref[j], vprev_ref[j]
                bias = bias_ref[variant, j]
            else:
                prev = slice((c - 1) * BLOCK, c * BLOCK)
                k_prev, v_prev = kn[j][:, prev], vv[j][:, prev]
                bias = bias_ref[0, j]
            k_band = jnp.concatenate([k_prev, kn[j][:, tok]], axis=1)
            v_band = jnp.concatenate([v_prev, vv[j][:, tok]], axis=1)
            q_grp = jnp.concatenate([qn[j * GROUP + hl][:, tok] for hl in range(GROUP)], axis=1)
            s = _dot_tn(k_band, q_grp) + bias
            out_proj_a(c * B_KV_HEADS + j)
            sink = jnp.concatenate(
                [jnp.full((1, BLOCK), sinks_ref[j * GROUP + hl], F32) for hl in range(GROUP)], axis=1)
            m = jnp.maximum(jnp.max(s, axis=0, keepdims=True), sink)
            p = jnp.exp(s - m)
            denom = jnp.sum(p, axis=0, keepdims=True) + jnp.exp(sink - m)
            o_t = _dot(v_band, p.astype(BF16)) / denom
            for hl in range(GROUP):
                head_rows[j * GROUP + hl] = o_t[:, hl * BLOCK:(hl + 1) * BLOCK]
        yb_blocks.append(jnp.concatenate(head_rows, axis=0))
    yb = jnp.concatenate(yb_blocks, axis=1).T

    last = slice((nblk - 1) * BLOCK, nblk * BLOCK)
    for j in range(B_KV_HEADS):
        kprev_ref[j] = kn[j][:, last]
        vprev_ref[j] = vv[j][:, last]

    o_ref[...] += gate * _dot(yb.astype(BF16), wout_ref[A_WIDTH:, :])


def _mix(x2, mod3, g_mix, w_uv, w_qkvt, w_out, spatial_w, sb_full, gv_row, gq_tab, gk_tab, sinks,
         bias_tab, *, seq, tm):
    n, d = x2.shape
    tiles_per_seq = seq // tm
    const2 = lambda i: (0, 0)
    return pl.pallas_call(
        functools.partial(_mix_kernel, tiles_per_seq=tiles_per_seq),
        grid=(n // tm,),
        in_specs=[
            pl.BlockSpec(memory_space=pltpu.SMEM),
            pl.BlockSpec((tm, d), lambda i: (i, 0)),
            pl.BlockSpec((1, N_MOD, d), lambda i: (i // tiles_per_seq, 0, 0)),
            pl.BlockSpec((1, d), const2),
            pl.BlockSpec(w_uv.shape, const2, pipeline_mode=pl.Buffered(1)),
            pl.BlockSpec(w_qkvt.shape, const2, pipeline_mode=pl.Buffered(1)),
            pl.BlockSpec(w_out.shape, const2, pipeline_mode=pl.Buffered(1)),
            pl.BlockSpec(spatial_w.shape, lambda i: (0, 0, 0), pipeline_mode=pl.Buffered(1)),
            pl.BlockSpec(sb_full.shape, const2, pipeline_mode=pl.Buffered(1)),
            pl.BlockSpec(gv_row.shape, const2),
            pl.BlockSpec(gq_tab.shape, const2),
            pl.BlockSpec(gk_tab.shape, const2),
            pl.BlockSpec(bias_tab.shape, lambda i: (0, 0, 0, 0), pipeline_mode=pl.Buffered(1)),
        ],
        out_specs=pl.BlockSpec((tm, d), lambda i: (i, 0)),
        scratch_shapes=[pltpu.VMEM((tm, d), BF16),
                        pltpu.VMEM((B_KV_HEADS, HEAD_DIM, BLOCK), BF16),
                        pltpu.VMEM((B_KV_HEADS, HEAD_DIM, BLOCK), BF16)],
        out_shape=jax.ShapeDtypeStruct((n, d), F32),
        compiler_params=pltpu.CompilerParams(
            dimension_semantics=("arbitrary",), vmem_limit_bytes=V7X_VMEM_LIMIT_BYTES),
        name="token_mix",
    )(sinks, x2, mod3, g_mix.reshape(1, d), w_uv, w_qkvt, w_out, spatial_w, sb_full, gv_row,
      gq_tab, gk_tab, bias_tab)


def _tile_sizes(seq, dff):
    tm_ffn = next(t for t in (1024, 512, 256, 128) if seq % t == 0)
    tf = next(t for t in (512, 256, 128) if dff % t == 0)
    tm_mix = next(t for t in (512, 256, 128) if seq % t == 0)
    return tm_ffn, tf, tm_mix


def kernel(x, c, w_ada, b_ada, g_ffn1, w1_ffn1, w3_ffn1, w2_ffn1, g_mix, w_in, spatial_w, spatial_b,
           g_v, g_q, g_k, sinks, rel_bias, w_out, g_ffn2, w1_ffn2, w3_ffn2, w2_ffn2):
    bsz, seq, d = x.shape
    depth = w_ada.shape[0]
    dff = w1_ffn1.shape[-1]
    assert seq % BLOCK == 0 and w_in.shape[-1] == 2 * A_WIDTH + QKV_WIDTH
    tm_ffn, tf, tm_mix = _tile_sizes(seq, dff)
    ada_tn = next(t for t in (1024, 512, 256, 128) if (N_MOD * d) % t == 0)

    bias_tab = _bias_table(rel_bias)
    x2 = x.reshape(bsz * seq, d)
    for l in range(depth):
        mod3 = _ada_mod(c, w_ada[l], b_ada[l], tn=ada_tn).reshape(bsz, N_MOD, d)
        x2 = _ffn(x2, mod3, g_ffn1[l], w1_ffn1[l].astype(BF16), w3_ffn1[l].astype(BF16),
                  w2_ffn1[l].astype(BF16), mod_row=0, seq=seq, tm=tm_ffn, tf=tf)
        w_in_l = w_in[l]
        x2 = _mix(
            x2, mod3, g_mix[l],
            w_in_l[:, :2 * A_WIDTH].astype(BF16),
            w_in_l[:, 2 * A_WIDTH:].T.astype(BF16),
            w_out[l].astype(BF16),
            spatial_w[l],
            jnp.repeat(spatial_b[l].T, A_HEAD_DIM, axis=1),
            g_v[l].reshape(1, A_WIDTH),
            jnp.broadcast_to(g_q[l][:, None], (HEAD_DIM, BLOCK)),
            jnp.broadcast_to(g_k[l][:, None], (HEAD_DIM, BLOCK)),
            sinks[l], bias_tab, seq=seq, tm=tm_mix)
        x2 = _ffn(x2, mod3, g_ffn2[l], w1_ffn2[l].astype(BF16), w3_ffn2[l].astype(BF16),
                  w2_ffn2[l].astype(BF16), mod_row=6, seq=seq, tm=tm_ffn, tf=tf)
    return x2.reshape(bsz, seq, d)
```

```python
import functools
import math

import jax
import jax.numpy as jnp
import numpy as np
from jax import lax
from jax.experimental import pallas as pl
from jax.experimental.pallas import tpu as pltpu

BLOCK = 128
A_HEADS = 8
A_HEAD_DIM = 128
A_WIDTH = A_HEADS * A_HEAD_DIM
B_HEADS = 16
B_KV_HEADS = 2
GROUP = B_HEADS // B_KV_HEADS
HEAD_DIM = 64
B_WIDTH = B_HEADS * HEAD_DIM
KV_WIDTH = B_KV_HEADS * HEAD_DIM
QKV_WIDTH = B_WIDTH + 2 * KV_WIDTH
N_BUCKETS = 32
MAX_DISTANCE = 128
N_MOD = 9
EPS = 1e-6
MASKED = -1e30

V7X_VMEM_LIMIT_BYTES = 60000 * 1024
V7X_VMEM_CAPACITY_BYTES = 64 * 1024 * 1024
FFN_VMEM_LIMIT_BYTES = V7X_VMEM_CAPACITY_BYTES - 1024 * 1024

F32 = jnp.float32
BF16 = jnp.bfloat16


def _dot(a, b):
    return jnp.dot(a, b, preferred_element_type=F32)


def _dot_nt(a, b):
    return lax.dot_general(a, b, (((1,), (1,)), ((), ())), preferred_element_type=F32)


def _dot_tn(a, b):
    return lax.dot_general(a, b, (((0,), (0,)), ((), ())), preferred_element_type=F32)


NORM_ROWS = 16


def _norm_modulate_into(h_ref, x_ref, g, shift, scale):
    gain = g * (1.0 + scale)
    for r0 in range(0, x_ref.shape[0], NORM_ROWS):
        rows = slice(r0, r0 + NORM_ROWS)
        x = x_ref[rows, :]
        ms = jnp.mean(x * x, axis=-1, keepdims=True)
        h_ref[rows, :] = (x * lax.rsqrt(ms + EPS) * gain + shift).astype(BF16)


def _ada_kernel(c_ref, w_ref, b_ref, o_ref):
    c = c_ref[...]
    c_act = (c * jax.nn.sigmoid(c)).astype(BF16)
    o_ref[...] = _dot(c_act, w_ref[...].astype(BF16)) + b_ref[...]


def _ada_mod(c, w_ada, b_ada, *, tn):
    bsz, d = c.shape
    n = w_ada.shape[1]
    return pl.pallas_call(
        _ada_kernel,
        grid=(n // tn,),
        in_specs=[
            pl.BlockSpec((bsz, d), lambda j: (0, 0)),
            pl.BlockSpec((d, tn), lambda j: (0, j)),
            pl.BlockSpec((1, tn), lambda j: (0, j)),
        ],
        out_specs=pl.BlockSpec((bsz, tn), lambda j: (0, j)),
        out_shape=jax.ShapeDtypeStruct((bsz, n), F32),
        compiler_params=pltpu.CompilerParams(
            dimension_semantics=("arbitrary",), vmem_limit_bytes=V7X_VMEM_LIMIT_BYTES),
        name="ada_mod",
    )(c, w_ada, b_ada.reshape(1, n))


def _ffn_kernel(x_hbm, mod_ref, g_ref, w1a_ref, w3a_ref, w2a_ref, w1b_ref, w3b_ref, w2b_ref, o_ref,
                x_buf, h_ref, x_sem, *, mod_row):
    i = pl.program_id(0)
    f = pl.program_id(1)
    tm = x_buf.shape[0]

    def x_copy(tile):
        return pltpu.make_async_copy(x_hbm.at[pl.ds(tile * tm, tm), :], x_buf, x_sem)

    def chunk(w1_ref, w3_ref, w2_ref):
        h = h_ref[...]
        a = _dot(h, w1_ref[...])
        b = _dot(h, w3_ref[...])
        p = (a * jax.nn.sigmoid(a) * b).astype(BF16)
        return (0.5 * mod_ref[0, mod_row + 2:mod_row + 3, :]) * _dot(p, w2_ref[...])

    @pl.when(f == 0)
    def _():
        @pl.when(i == 0)
        def _():
            x_copy(0).start()

        x_copy(i).wait()
        _norm_modulate_into(h_ref, x_buf, g_ref[...], mod_ref[0, mod_row:mod_row + 1, :],
                            mod_ref[0, mod_row + 1:mod_row + 2, :])
        o_ref[...] = x_buf[...] + chunk(w1a_ref, w3a_ref, w2a_ref)

    @pl.when(jnp.logical_and(f == 1, i + 1 < pl.num_programs(0)))
    def _():
        x_copy(i + 1).start()

    @pl.when(f != 0)
    def _():
        o_ref[...] += chunk(w1a_ref, w3a_ref, w2a_ref)
        o_ref[...] += chunk(w1b_ref, w3b_ref, w2b_ref)


def _ffn(x2, mod3, g, w1, w3, w2, *, mod_row, seq, tm, tf):
    n, d = x2.shape
    nf = w1.shape[1] // tf
    assert nf % 2 == 1 and nf >= 3, "one chunk at step 0, then pairs"
    tiles_per_seq = seq // tm
    first_of_pair = lambda f: jnp.maximum(2 * f - 1, 0)
    second_of_pair = lambda f: jnp.maximum(2 * f, 2)
    up_a = pl.BlockSpec((d, tf), lambda i, f: (0, first_of_pair(f)))
    up_b = pl.BlockSpec((d, tf), lambda i, f: (0, second_of_pair(f)))
    down_a = pl.BlockSpec((tf, d), lambda i, f: (first_of_pair(f), 0))
    down_b = pl.BlockSpec((tf, d), lambda i, f: (second_of_pair(f), 0))
    return pl.pallas_call(
        functools.partial(_ffn_kernel, mod_row=mod_row),
        grid=(n // tm, (nf + 1) // 2),
        in_specs=[
            pl.BlockSpec(memory_space=pl.ANY),
            pl.BlockSpec((1, N_MOD, d), lambda i, f: (i // tiles_per_seq, 0, 0)),
            pl.BlockSpec((1, d), lambda i, f: (0, 0)),
            up_a, up_a, down_a, up_b, up_b, down_b,
        ],
        out_specs=pl.BlockSpec((tm, d), lambda i, f: (i, 0)),
        out_shape=jax.ShapeDtypeStruct((n, d), F32),
        scratch_shapes=[pltpu.VMEM((tm, d), F32), pltpu.VMEM((tm, d), BF16),
                        pltpu.SemaphoreType.DMA(())],
        compiler_params=pltpu.CompilerParams(
            dimension_semantics=("arbitrary", "arbitrary"),
            vmem_limit_bytes=FFN_VMEM_LIMIT_BYTES),
        name=f"ffn_mod{mod_row}",
    )(x2, mod3, g.reshape(1, d), w1, w3, w2, w1, w3, w2)


def _t5_bucket_table():
    kj = np.arange(2 * BLOCK)[:, None]
    qi = np.arange(BLOCK)[None, :]
    dist = qi + BLOCK - kj
    in_window = (dist >= 0) & (dist < BLOCK)
    n = np.clip(dist, 0, None)
    max_exact = N_BUCKETS // 2
    nf = np.maximum(n, 1).astype(np.float32)
    large = max_exact + (np.log(nf / np.float32(max_exact)) / np.float32(math.log(MAX_DISTANCE / max_exact))
                         * np.float32(N_BUCKETS - max_exact)).astype(np.int32)
    large = np.minimum(large, N_BUCKETS - 1)
    bucket = np.where(n < max_exact, n, large)
    return np.where(in_window, bucket, -1).astype(np.int32)


def _bias_kernel(rb_ref, bucket_ref, o_ref):
    bucket = bucket_ref[...]
    prev_rows = lax.broadcasted_iota(jnp.int32, bucket.shape, 0) < BLOCK
    for head in range(B_HEADS):
        acc = jnp.full(bucket.shape, MASKED, F32)
        for b in range(N_BUCKETS):
            acc = jnp.where(bucket == b, rb_ref[b, head], acc)
        j, hl = divmod(head, GROUP)
        o_ref[0, j, :, hl * BLOCK:(hl + 1) * BLOCK] = acc
        o_ref[1, j, :, hl * BLOCK:(hl + 1) * BLOCK] = jnp.where(prev_rows, MASKED, acc)


def _bias_table(rel_bias):
    return pl.pallas_call(
        _bias_kernel,
        in_specs=[
            pl.BlockSpec(memory_space=pltpu.SMEM),
            pl.BlockSpec((2 * BLOCK, BLOCK), lambda: (0, 0)),
        ],
        out_specs=pl.BlockSpec((2, B_KV_HEADS, 2 * BLOCK, GROUP * BLOCK), lambda: (0, 0, 0, 0)),
        out_shape=jax.ShapeDtypeStruct((2, B_KV_HEADS, 2 * BLOCK, GROUP * BLOCK), F32),
        name="rel_bias_table",
    )(rel_bias, jnp.asarray(_t5_bucket_table()))


def _gelu(x):
    return 0.5 * x * (1.0 + lax.erf(x * (1.0 / math.sqrt(2.0))))


def _rms_rows(xt, gain):
    ms = jnp.mean(xt * xt, axis=0, keepdims=True)
    return xt * lax.rsqrt(ms + EPS) * gain


def _mix_kernel(sinks_ref, x_ref, mod_ref, g_ref, wuv_ref, wqkvt_ref, wout_ref, sw_ref, sb_ref,
                gv_ref, gq_ref, gk_ref, bias_ref, o_ref, h_ref, kprev_ref, vprev_ref, *, tiles_per_seq):
    tm = x_ref.shape[0]
    nblk = tm // BLOCK
    first = pl.program_id(0) % tiles_per_seq == 0

    @pl.when(first)
    def _():
        kprev_ref[...] = jnp.zeros_like(kprev_ref)
        vprev_ref[...] = jnp.zeros_like(vprev_ref)

    _norm_modulate_into(h_ref, x_ref, g_ref[...], mod_ref[0, 3:4, :], mod_ref[0, 4:5, :])
    h = h_ref[...]

    zuv = _dot(h, wuv_ref[...])
    u = _gelu(zuv[:, :A_WIDTH])
    v = _gelu(zuv[:, A_WIDTH:])
    row = lax.broadcasted_iota(jnp.int32, (BLOCK, BLOCK), 0)
    col = lax.broadcasted_iota(jnp.int32, (BLOCK, BLOCK), 1)
    causal = row >= col
    ya_heads = []
    for hd in range(A_HEADS):
        cols = slice(hd * A_HEAD_DIM, (hd + 1) * A_HEAD_DIM)
        vh = v[:, cols]
        ms = jnp.mean(vh * vh, axis=-1, keepdims=True)
        vn = (vh * lax.rsqrt(ms + EPS) * gv_ref[:, cols]).astype(BF16)
        w_tril = jnp.where(causal, sw_ref[hd], 0.0).astype(BF16)
        mixed = [_dot(w_tril, vn[c * BLOCK:(c + 1) * BLOCK]) + sb_ref[:, cols] for c in range(nblk)]
        ya_heads.append(u[:, cols] * jnp.concatenate(mixed, axis=0))
    ya = jnp.concatenate(ya_heads, axis=1).astype(BF16)

    gate = mod_ref[0, 5:6, :]
    d_model = o_ref.shape[1]
    slab = max(d_model // (nblk * B_KV_HEADS), BLOCK)
    n_slabs = d_model // slab

    def out_proj_a(unit):
        if unit < n_slabs:
            cols = slice(unit * slab, (unit + 1) * slab)
            o_ref[:, cols] = x_ref[:, cols] + gate[:, cols] * _dot(ya, wout_ref[:A_WIDTH, cols])

    zt = _dot_nt(wqkvt_ref[...], h)
    gq = gq_ref[...] * (HEAD_DIM ** -0.5)
    qn = [_rms_rows(zt[hd * HEAD_DIM:(hd + 1) * HEAD_DIM], jnp.tile(gq, (1, nblk))).astype(BF16)
          for hd in range(B_HEADS)]
    kn, vv = [], []
    for j in range(B_KV_HEADS):
        k_rows = slice(B_WIDTH + j * HEAD_DIM, B_WIDTH + (j + 1) * HEAD_DIM)
        v_rows = slice(B_WIDTH + KV_WIDTH + j * HEAD_DIM, B_WIDTH + KV_WIDTH + (j + 1) * HEAD_DIM)
        kn.append(_rms_rows(zt[k_rows], jnp.tile(gk_ref[...], (1, nblk))).astype(BF16))
        vv.append(zt[v_rows].astype(BF16))

    variant = jnp.where(first, 1, 0)
    yb_blocks = []
    for c in range(nblk):
        tok = slice(c * BLOCK, (c + 1) * BLOCK)
        head_rows = [None] * B_HEADS
        for j in range(B_KV_HEADS):
            if c == 0:
                k_prev, v_prev = kprev_ref[j], vprev_ref[j]
                bias = bias_ref[variant, j]
            else:
                prev = slice((c - 1) * BLOCK, c * BLOCK)
                k_prev, v_prev = kn[j][:, prev], vv[j][:, prev]
                bias = bias_ref[0, j]
            k_band = jnp.concatenate([k_prev, kn[j][:, tok]], axis=1)
            v_band = jnp.concatenate([v_prev, vv[j][:, tok]], axis=1)
            q_grp = jnp.concatenate([qn[j * GROUP + hl][:, tok] for hl in range(GROUP)], axis=1)
            s = _dot_tn(k_band, q_grp) + bias
            out_proj_a(c * B_KV_HEADS + j)
            sink = jnp.concatenate(
                [jnp.full((1, BLOCK), sinks_ref[j * GROUP + hl], F32) for hl in range(GROUP)], axis=1)
            m = jnp.maximum(jnp.max(s, axis=0, keepdims=True), sink)
            p = jnp.exp(s - m)
            denom = jnp.sum(p, axis=0, keepdims=True) + jnp.exp(sink - m)
            o_t = _dot(v_band, p.astype(BF16)) / denom
            for hl in range(GROUP):
                head_rows[j * GROUP + hl] = o_t[:, hl * BLOCK:(hl + 1) * BLOCK]
        yb_blocks.append(jnp.concatenate(head_rows, axis=0))
    yb = jnp.concatenate(yb_blocks, axis=1).T

    last = slice((nblk - 1) * BLOCK, nblk * BLOCK)
    for j in range(B_KV_HEADS):
        kprev_ref[j] = kn[j][:, last]
        vprev_ref[j] = vv[j][:, last]

    o_ref[...] += gate * _dot(yb.astype(BF16), wout_ref[A_WIDTH:, :])


def _mix(x2, mod3, g_mix, w_uv, w_qkvt, w_out, spatial_w, sb_full, gv_row, gq_tab, gk_tab, sinks,
         bias_tab, *, seq, tm):
    n, d = x2.shape
    tiles_per_seq = seq // tm
    const2 = lambda i: (0, 0)
    return pl.pallas_call(
        functools.partial(_mix_kernel, tiles_per_seq=tiles_per_seq),
        grid=(n // tm,),
        in_specs=[
            pl.BlockSpec(memory_space=pltpu.SMEM),
            pl.BlockSpec((tm, d), lambda i: (i, 0)),
            pl.BlockSpec((1, N_MOD, d), lambda i: (i // tiles_per_seq, 0, 0)),
            pl.BlockSpec((1, d), const2),
            pl.BlockSpec(w_uv.shape, const2, pipeline_mode=pl.Buffered(1)),
            pl.BlockSpec(w_qkvt.shape, const2, pipeline_mode=pl.Buffered(1)),
            pl.BlockSpec(w_out.shape, const2, pipeline_mode=pl.Buffered(1)),
            pl.BlockSpec(spatial_w.shape, lambda i: (0, 0, 0), pipeline_mode=pl.Buffered(1)),
            pl.BlockSpec(sb_full.shape, const2, pipeline_mode=pl.Buffered(1)),
            pl.BlockSpec(gv_row.shape, const2),
            pl.BlockSpec(gq_tab.shape, const2),
            pl.BlockSpec(gk_tab.shape, const2),
            pl.BlockSpec(bias_tab.shape, lambda i: (0, 0, 0, 0), pipeline_mode=pl.Buffered(1)),
        ],
        out_specs=pl.BlockSpec((tm, d), lambda i: (i, 0)),
        scratch_shapes=[pltpu.VMEM((tm, d), BF16),
                        pltpu.VMEM((B_KV_HEADS, HEAD_DIM, BLOCK), BF16),
                        pltpu.VMEM((B_KV_HEADS, HEAD_DIM, BLOCK), BF16)],
        out_shape=jax.ShapeDtypeStruct((n, d), F32),
        compiler_params=pltpu.CompilerParams(
            dimension_semantics=("arbitrary",), vmem_limit_bytes=V7X_VMEM_LIMIT_BYTES),
        name="token_mix",
    )(sinks, x2, mod3, g_mix.reshape(1, d), w_uv, w_qkvt, w_out, spatial_w, sb_full, gv_row,
      gq_tab, gk_tab, bias_tab)


def _tile_sizes(seq, dff):
    tm_ffn = next(t for t in (1024, 512, 256, 128) if seq % t == 0)
    tf = next(t for t in (512, 256, 128) if dff % t == 0)
    tm_mix = next(t for t in (512, 256, 128) if seq % t == 0)
    return tm_ffn, tf, tm_mix


def kernel(x, c, w_ada, b_ada, g_ffn1, w1_ffn1, w3_ffn1, w2_ffn1, g_mix, w_in, spatial_w, spatial_b,
           g_v, g_q, g_k, sinks, rel_bias, w_out, g_ffn2, w1_ffn2, w3_ffn2, w2_ffn2):
    bsz, seq, d = x.shape
    depth = w_ada.shape[0]
    dff = w1_ffn1.shape[-1]
    assert seq % BLOCK == 0 and w_in.shape[-1] == 2 * A_WIDTH + QKV_WIDTH
    tm_ffn, tf, tm_mix = _tile_sizes(seq, dff)
    ada_tn = next(t for t in (1024, 512, 256, 128) if (N_MOD * d) % t == 0)

    bias_tab = _bias_table(rel_bias)
    x2 = x.reshape(bsz * seq, d)
    for l in range(depth):
        mod3 = _ada_mod(c, w_ada[l], b_ada[l], tn=ada_tn).reshape(bsz, N_MOD, d)
        x2 = _ffn(x2, mod3, g_ffn1[l], w1_ffn1[l].astype(BF16), w3_ffn1[l].astype(BF16),
                  w2_ffn1[l].astype(BF16), mod_row=0, seq=seq, tm=tm_ffn, tf=tf)
        w_in_l = w_in[l]
        x2 = _mix(
            x2, mod3, g_mix[l],
            w_in_l[:, :2 * A_WIDTH].astype(BF16),
            w_in_l[:, 2 * A_WIDTH:].T.astype(BF16),
            w_out[l].astype(BF16),
            spatial_w[l],
            jnp.repeat(spatial_b[l].T, A_HEAD_DIM, axis=1),
            g_v[l].reshape(1, A_WIDTH),
            jnp.broadcast_to(g_q[l][:, None], (HEAD_DIM, BLOCK)),
            jnp.broadcast_to(g_k[l][:, None], (HEAD_DIM, BLOCK)),
            sinks[l], bias_tab, seq=seq, tm=tm_mix)
        x2 = _ffn(x2, mod3, g_ffn2[l], w1_ffn2[l].astype(BF16), w3_ffn2[l].astype(BF16),
                  w2_ffn2[l].astype(BF16), mod_row=6, seq=seq, tm=tm_ffn, tf=tf)
    return x2.reshape(bsz, seq, d)
```

```python
import functools
import math

import jax
import jax.numpy as jnp
import numpy as np
from jax import lax
from jax.experimental import pallas as pl
from jax.experimental.pallas import tpu as pltpu

BLOCK = 128
A_HEADS = 8
A_HEAD_DIM = 128
A_WIDTH = A_HEADS * A_HEAD_DIM
B_HEADS = 16
B_KV_HEADS = 2
GROUP = B_HEADS // B_KV_HEADS
HEAD_DIM = 64
B_WIDTH = B_HEADS * HEAD_DIM
KV_WIDTH = B_KV_HEADS * HEAD_DIM
QKV_WIDTH = B_WIDTH + 2 * KV_WIDTH
N_BUCKETS = 32
MAX_DISTANCE = 128
N_MOD = 9
EPS = 1e-6
MASKED = -1e30

V7X_VMEM_LIMIT_BYTES = 60000 * 1024
V7X_VMEM_CAPACITY_BYTES = 64 * 1024 * 1024
BIG_VMEM_LIMIT_BYTES = V7X_VMEM_CAPACITY_BYTES - 1024 * 1024

F32 = jnp.float32
BF16 = jnp.bfloat16


def _dot(a, b):
    return jnp.dot(a, b, preferred_element_type=F32)


def _dot_nt(a, b):
    return lax.dot_general(a, b, (((1,), (1,)), ((), ())), preferred_element_type=F32)


def _dot_tn(a, b):
    return lax.dot_general(a, b, (((0,), (0,)), ((), ())), preferred_element_type=F32)


NORM_ROWS = 16


def _norm_modulate_into(h_ref, x_ref, g, shift, scale):
    gain = g * (1.0 + scale)
    for r0 in range(0, x_ref.shape[0], NORM_ROWS):
        rows = slice(r0, r0 + NORM_ROWS)
        x = x_ref[rows, :]
        ms = jnp.mean(x * x, axis=-1, keepdims=True)
        h_ref[rows, :] = (x * lax.rsqrt(ms + EPS) * gain + shift).astype(BF16)


def _ada_kernel(c_ref, w_ref, b_ref, o_ref):
    c = c_ref[...]
    c_act = (c * jax.nn.sigmoid(c)).astype(BF16)
    o_ref[...] = _dot(c_act, w_ref[...].astype(BF16)) + b_ref[...]


def _ada_mod(c, w_ada, b_ada, *, tn):
    bsz, d = c.shape
    n = w_ada.shape[1]
    return pl.pallas_call(
        _ada_kernel,
        grid=(n // tn,),
        in_specs=[
            pl.BlockSpec((bsz, d), lambda j: (0, 0)),
            pl.BlockSpec((d, tn), lambda j: (0, j)),
            pl.BlockSpec((1, tn), lambda j: (0, j)),
        ],
        out_specs=pl.BlockSpec((bsz, tn), lambda j: (0, j)),
        out_shape=jax.ShapeDtypeStruct((bsz, n), F32),
        compiler_params=pltpu.CompilerParams(
            dimension_semantics=("arbitrary",), vmem_limit_bytes=V7X_VMEM_LIMIT_BYTES),
        name="ada_mod",
    )(c, w_ada, b_ada.reshape(1, n))


def _ffn_kernel(x_hbm, mod_ref, g_ref, w1a_ref, w3a_ref, w2a_ref, w1b_ref, w3b_ref, w2b_ref, o_ref,
                x_buf, h_ref, x_sem, *, mod_row):
    i = pl.program_id(0)
    f = pl.program_id(1)
    tm = x_buf.shape[0]

    def x_copy(tile):
        return pltpu.make_async_copy(x_hbm.at[pl.ds(tile * tm, tm), :], x_buf, x_sem)

    def chunk(w1_ref, w3_ref, w2_ref):
        h = h_ref[...]
        a = _dot(h, w1_ref[...])
        b = _dot(h, w3_ref[...])
        p = (a * jax.nn.sigmoid(a) * b).astype(BF16)
        return (0.5 * mod_ref[0, mod_row + 2:mod_row + 3, :]) * _dot(p, w2_ref[...])

    @pl.when(f == 0)
    def _():
        @pl.when(i == 0)
        def _():
            x_copy(0).start()

        x_copy(i).wait()
        _norm_modulate_into(h_ref, x_buf, g_ref[...], mod_ref[0, mod_row:mod_row + 1, :],
                            mod_ref[0, mod_row + 1:mod_row + 2, :])
        o_ref[...] = x_buf[...] + chunk(w1a_ref, w3a_ref, w2a_ref)

    @pl.when(jnp.logical_and(f == 1, i + 1 < pl.num_programs(0)))
    def _():
        x_copy(i + 1).start()

    @pl.when(f != 0)
    def _():
        o_ref[...] += chunk(w1a_ref, w3a_ref, w2a_ref)
        o_ref[...] += chunk(w1b_ref, w3b_ref, w2b_ref)


def _ffn(x2, mod3, g, w1, w3, w2, *, mod_row, seq, tm, tf):
    n, d = x2.shape
    nf = w1.shape[1] // tf
    assert nf % 2 == 1 and nf >= 3, "one chunk at step 0, then pairs"
    tiles_per_seq = seq // tm
    first_of_pair = lambda f: jnp.maximum(2 * f - 1, 0)
    second_of_pair = lambda f: jnp.maximum(2 * f, 2)
    up_a = pl.BlockSpec((d, tf), lambda i, f: (0, first_of_pair(f)))
    up_b = pl.BlockSpec((d, tf), lambda i, f: (0, second_of_pair(f)))
    down_a = pl.BlockSpec((tf, d), lambda i, f: (first_of_pair(f), 0))
    down_b = pl.BlockSpec((tf, d), lambda i, f: (second_of_pair(f), 0))
    return pl.pallas_call(
        functools.partial(_ffn_kernel, mod_row=mod_row),
        grid=(n // tm, (nf + 1) // 2),
        in_specs=[
            pl.BlockSpec(memory_space=pl.ANY),
            pl.BlockSpec((1, N_MOD, d), lambda i, f: (i // tiles_per_seq, 0, 0)),
            pl.BlockSpec((1, d), lambda i, f: (0, 0)),
            up_a, up_a, down_a, up_b, up_b, down_b,
        ],
        out_specs=pl.BlockSpec((tm, d), lambda i, f: (i, 0)),
        out_shape=jax.ShapeDtypeStruct((n, d), F32),
        scratch_shapes=[pltpu.VMEM((tm, d), F32), pltpu.VMEM((tm, d), BF16),
                        pltpu.SemaphoreType.DMA(())],
        compiler_params=pltpu.CompilerParams(
            dimension_semantics=("arbitrary", "arbitrary"),
            vmem_limit_bytes=BIG_VMEM_LIMIT_BYTES),
        name=f"ffn_mod{mod_row}",
    )(x2, mod3, g.reshape(1, d), w1, w3, w2, w1, w3, w2)


def _t5_bucket_table():
    kj = np.arange(2 * BLOCK)[:, None]
    qi = np.arange(BLOCK)[None, :]
    dist = qi + BLOCK - kj
    in_window = (dist >= 0) & (dist < BLOCK)
    n = np.clip(dist, 0, None)
    max_exact = N_BUCKETS // 2
    nf = np.maximum(n, 1).astype(np.float32)
    large = max_exact + (np.log(nf / np.float32(max_exact)) / np.float32(math.log(MAX_DISTANCE / max_exact))
                         * np.float32(N_BUCKETS - max_exact)).astype(np.int32)
    large = np.minimum(large, N_BUCKETS - 1)
    bucket = np.where(n < max_exact, n, large)
    return np.where(in_window, bucket, -1).astype(np.int32)


def _bias_kernel(rb_ref, bucket_ref, o_ref):
    bucket = bucket_ref[...]
    prev_rows = lax.broadcasted_iota(jnp.int32, bucket.shape, 0) < BLOCK
    for head in range(B_HEADS):
        acc = jnp.full(bucket.shape, MASKED, F32)
        for b in range(N_BUCKETS):
            acc = jnp.where(bucket == b, rb_ref[b, head], acc)
        j, hl = divmod(head, GROUP)
        o_ref[0, j, :, hl * BLOCK:(hl + 1) * BLOCK] = acc
        o_ref[1, j, :, hl * BLOCK:(hl + 1) * BLOCK] = jnp.where(prev_rows, MASKED, acc)


def _bias_table(rel_bias):
    return pl.pallas_call(
        _bias_kernel,
        in_specs=[
            pl.BlockSpec(memory_space=pltpu.SMEM),
            pl.BlockSpec((2 * BLOCK, BLOCK), lambda: (0, 0)),
        ],
        out_specs=pl.BlockSpec((2, B_KV_HEADS, 2 * BLOCK, GROUP * BLOCK), lambda: (0, 0, 0, 0)),
        out_shape=jax.ShapeDtypeStruct((2, B_KV_HEADS, 2 * BLOCK, GROUP * BLOCK), F32),
        name="rel_bias_table",
    )(rel_bias, jnp.asarray(_t5_bucket_table()))


def _gelu(x):
    return 0.5 * x * (1.0 + lax.erf(x * (1.0 / math.sqrt(2.0))))


def _rms_rows(xt, gain):
    ms = jnp.mean(xt * xt, axis=0, keepdims=True)
    return xt * lax.rsqrt(ms + EPS) * gain


def _mix_kernel(sinks_ref, x_ref, mod_ref, g_ref, wuv_ref, wqkvt_ref, wout_ref, sw_ref, sb_ref,
                gv_ref, gq_ref, gk_ref, bias_ref, cast_a_ref, cast_b_ref, cast_c_ref,
                o_ref, cast_a_out, cast_b_out, cast_c_out, h_ref, kprev_ref, vprev_ref, *, tiles_per_seq):
    for src, dst in ((cast_a_ref, cast_a_out), (cast_b_ref, cast_b_out), (cast_c_ref, cast_c_out)):
        dst[...] = src[...].astype(BF16)

    tm = x_ref.shape[0]
    nblk = tm // BLOCK
    first = pl.program_id(0) % tiles_per_seq == 0

    @pl.when(first)
    def _():
        kprev_ref[...] = jnp.zeros_like(kprev_ref)
        vprev_ref[...] = jnp.zeros_like(vprev_ref)

    _norm_modulate_into(h_ref, x_ref, g_ref[...], mod_ref[0, 3:4, :], mod_ref[0, 4:5, :])
    h = h_ref[...]

    zuv = _dot(h, wuv_ref[...])
    u = _gelu(zuv[:, :A_WIDTH])
    v = _gelu(zuv[:, A_WIDTH:])
    row = lax.broadcasted_iota(jnp.int32, (BLOCK, BLOCK), 0)
    col = lax.broadcasted_iota(jnp.int32, (BLOCK, BLOCK), 1)
    causal = row >= col
    ya_heads = []
    for hd in range(A_HEADS):
        cols = slice(hd * A_HEAD_DIM, (hd + 1) * A_HEAD_DIM)
        vh = v[:, cols]
        ms = jnp.mean(vh * vh, axis=-1, keepdims=True)
        vn = (vh * lax.rsqrt(ms + EPS) * gv_ref[:, cols]).astype(BF16)
        w_tril = jnp.where(causal, sw_ref[hd], 0.0).astype(BF16)
        mixed = [_dot(w_tril, vn[c * BLOCK:(c + 1) * BLOCK]) + sb_ref[:, cols] for c in range(nblk)]
        ya_heads.append(u[:, cols] * jnp.concatenate(mixed, axis=0))
    ya = jnp.concatenate(ya_heads, axis=1).astype(BF16)

    gate = mod_ref[0, 5:6, :]
    d_model = o_ref.shape[1]
    slab = max(d_model // (nblk * B_KV_HEADS), BLOCK)
    n_slabs = d_model // slab

    def out_proj_a(unit):
        if unit < n_slabs:
            cols = slice(unit * slab, (unit + 1) * slab)
            o_ref[:, cols] = x_ref[:, cols] + gate[:, cols] * _dot(ya, wout_ref[:A_WIDTH, cols])

    zt = _dot_nt(wqkvt_ref[...], h)
    gq = gq_ref[...] * (HEAD_DIM ** -0.5)
    qn = [_rms_rows(zt[hd * HEAD_DIM:(hd + 1) * HEAD_DIM], jnp.tile(gq, (1, nblk))).astype(BF16)
          for hd in range(B_HEADS)]
    kn, vv = [], []
    for j in range(B_KV_HEADS):
        k_rows = slice(B_WIDTH + j * HEAD_DIM, B_WIDTH + (j + 1) * HEAD_DIM)
        v_rows = slice(B_WIDTH + KV_WIDTH + j * HEAD_DIM, B_WIDTH + KV_WIDTH + (j + 1) * HEAD_DIM)
        kn.append(_rms_rows(zt[k_rows], jnp.tile(gk_ref[...], (1, nblk))).astype(BF16))
        vv.append(zt[v_rows].astype(BF16))

    variant = jnp.where(first, 1, 0)
    yb_blocks = []
    for c in range(nblk):
        tok = slice(c * BLOCK, (c + 1) * BLOCK)
        head_rows = [None] * B_HEADS
        for j in range(B_KV_HEADS):
            if c == 0:
                k_prev, v_prev = kprev_ref[j], vprev_ref[j]
                bias = bias_ref[variant, j]
            else:
                prev = slice((c - 1) * BLOCK, c * BLOCK)
                k_prev, v_prev = kn[j][:, prev], vv[j][:, prev]
                bias = bias_ref[0, j]
            k_band = jnp.concatenate([k_prev, kn[j][:, tok]], axis=1)
            v_band = jnp.concatenate([v_prev, vv[j][:, tok]], axis=1)
            q_grp = jnp.concatenate([qn[j * GROUP + hl][:, tok] for hl in range(GROUP)], axis=1)
            s = _dot_tn(k_band, q_grp) + bias
            out_proj_a(c * B_KV_HEADS + j)
            sink = jnp.concatenate(
                [jnp.full((1, BLOCK), sinks_ref[j * GROUP + hl], F32) for hl in range(GROUP)], axis=1)
            m = jnp.maximum(jnp.max(s, axis=0, keepdims=True), sink)
            p = jnp.exp(s - m)
            denom = jnp.sum(p, axis=0, keepdims=True) + jnp.exp(sink - m)
            o_t = _dot(v_band, p.astype(BF16)) / denom
            for hl in range(GROUP):
                head_rows[j * GROUP + hl] = o_t[:, hl * BLOCK:(hl + 1) * BLOCK]
        yb_blocks.append(jnp.concatenate(head_rows, axis=0))
    yb = jnp.concatenate(yb_blocks, axis=1).T

    last = slice((nblk - 1) * BLOCK, nblk * BLOCK)
    for j in range(B_KV_HEADS):
        kprev_ref[j] = kn[j][:, last]
        vprev_ref[j] = vv[j][:, last]

    o_ref[...] += gate * _dot(yb.astype(BF16), wout_ref[A_WIDTH:, :])


def _cast_block_spec(shape, steps):
    rows, cols = shape
    for row_parts in range(steps, 0, -1):
        col_parts = steps // row_parts
        if (row_parts * col_parts == steps and rows % (16 * row_parts) == 0
                and cols % (128 * col_parts) == 0):
            return pl.BlockSpec((rows // row_parts, cols // col_parts),
                                lambda i: (i // col_parts, i % col_parts))
    raise ValueError(f"no {steps}-step tiling of {shape}")


def _mix(x2, mod3, g_mix, w_in_bf, w_qkvt, w_out, spatial_w, sb_full, gv_row, gq_tab, gk_tab, sinks,
         bias_tab, to_cast, *, seq, tm):
    n, d = x2.shape
    tiles_per_seq = seq // tm
    steps = n // tm
    const2 = lambda i: (0, 0)
    cast_specs = [_cast_block_spec(w.shape, steps) for w in to_cast]
    return pl.pallas_call(
        functools.partial(_mix_kernel, tiles_per_seq=tiles_per_seq),
        grid=(n // tm,),
        in_specs=[
            pl.BlockSpec(memory_space=pltpu.SMEM),
            pl.BlockSpec((tm, d), lambda i: (i, 0)),
            pl.BlockSpec((1, N_MOD, d), lambda i: (i // tiles_per_seq, 0, 0)),
            pl.BlockSpec((1, d), const2),
            pl.BlockSpec((d, 2 * A_WIDTH), const2, pipeline_mode=pl.Buffered(1)),
            pl.BlockSpec(w_qkvt.shape, const2, pipeline_mode=pl.Buffered(1)),
            pl.BlockSpec(w_out.shape, const2, pipeline_mode=pl.Buffered(1)),
            pl.BlockSpec(spatial_w.shape, lambda i: (0, 0, 0), pipeline_mode=pl.Buffered(1)),
            pl.BlockSpec(sb_full.shape, const2, pipeline_mode=pl.Buffered(1)),
            pl.BlockSpec(gv_row.shape, const2),
            pl.BlockSpec(gq_tab.shape, const2),
            pl.BlockSpec(gk_tab.shape, const2),
            pl.BlockSpec(bias_tab.shape, lambda i: (0, 0, 0, 0), pipeline_mode=pl.Buffered(1)),
            *cast_specs,
        ],
        out_specs=[pl.BlockSpec((tm, d), lambda i: (i, 0)), *cast_specs],
        scratch_shapes=[pltpu.VMEM((tm, d), BF16),
                        pltpu.VMEM((B_KV_HEADS, HEAD_DIM, BLOCK), BF16),
                        pltpu.VMEM((B_KV_HEADS, HEAD_DIM, BLOCK), BF16)],
        out_shape=[jax.ShapeDtypeStruct((n, d), F32)]
        + [jax.ShapeDtypeStruct(w.shape, BF16) for w in to_cast],
        compiler_params=pltpu.CompilerParams(
            dimension_semantics=("arbitrary",), vmem_limit_bytes=BIG_VMEM_LIMIT_BYTES),
        name="token_mix",
    )(sinks, x2, mod3, g_mix.reshape(1, d), w_in_bf, w_qkvt, w_out, spatial_w, sb_full, gv_row,
      gq_tab, gk_tab, bias_tab, *to_cast)


def _tile_sizes(seq, dff):
    tm_ffn = next(t for t in (1024, 512, 256, 128) if seq % t == 0)
    tf = next(t for t in (512, 256, 128) if dff % t == 0)
    tm_mix = next(t for t in (512, 256, 128) if seq % t == 0)
    return tm_ffn, tf, tm_mix


def kernel(x, c, w_ada, b_ada, g_ffn1, w1_ffn1, w3_ffn1, w2_ffn1, g_mix, w_in, spatial_w, spatial_b,
           g_v, g_q, g_k, sinks, rel_bias, w_out, g_ffn2, w1_ffn2, w3_ffn2, w2_ffn2):
    bsz, seq, d = x.shape
    depth = w_ada.shape[0]
    dff = w1_ffn1.shape[-1]
    assert seq % BLOCK == 0 and w_in.shape[-1] == 2 * A_WIDTH + QKV_WIDTH
    tm_ffn, tf, tm_mix = _tile_sizes(seq, dff)
    ada_tn = next(t for t in (1024, 512, 256, 128) if (N_MOD * d) % t == 0)

    bias_tab = _bias_table(rel_bias)
    x2 = x.reshape(bsz * seq, d)
    for l in range(depth):
        mod3 = _ada_mod(c, w_ada[l], b_ada[l], tn=ada_tn).reshape(bsz, N_MOD, d)
        x2 = _ffn(x2, mod3, g_ffn1[l], w1_ffn1[l].astype(BF16), w3_ffn1[l].astype(BF16),
                  w2_ffn1[l].astype(BF16), mod_row=0, seq=seq, tm=tm_ffn, tf=tf)
        w_in_bf = w_in[l].astype(BF16)
        x2, w1b, w3b, w2b = _mix(
            x2, mod3, g_mix[l], w_in_bf, w_in_bf[:, 2 * A_WIDTH:].T, w_out[l].astype(BF16),
            spatial_w[l],
            jnp.repeat(spatial_b[l].T, A_HEAD_DIM, axis=1),
            g_v[l].reshape(1, A_WIDTH),
            jnp.broadcast_to(g_q[l][:, None], (HEAD_DIM, BLOCK)),
            jnp.broadcast_to(g_k[l][:, None], (HEAD_DIM, BLOCK)),
            sinks[l], bias_tab,
            (w1_ffn2[l], w3_ffn2[l], w2_ffn2[l]), seq=seq, tm=tm_mix)
        x2 = _ffn(x2, mod3, g_ffn2[l], w1b, w3b, w2b,
                  mod_row=6, seq=seq, tm=tm_ffn, tf=tf)
    return x2.reshape(bsz, seq, d)
```

```python
import functools
import math

import jax
import jax.numpy as jnp
import numpy as np
from jax import lax
from jax.experimental import pallas as pl
from jax.experimental.pallas import tpu as pltpu

BLOCK = 128
A_HEADS = 8
A_HEAD_DIM = 128
A_WIDTH = A_HEADS * A_HEAD_DIM
B_HEADS = 16
B_KV_HEADS = 2
GROUP = B_HEADS // B_KV_HEADS
HEAD_DIM = 64
B_WIDTH = B_HEADS * HEAD_DIM
KV_WIDTH = B_KV_HEADS * HEAD_DIM
QKV_WIDTH = B_WIDTH + 2 * KV_WIDTH
N_BUCKETS = 32
MAX_DISTANCE = 128
N_MOD = 9
EPS = 1e-6
MASKED = -1e30

V7X_VMEM_LIMIT_BYTES = 60000 * 1024
V7X_VMEM_CAPACITY_BYTES = 64 * 1024 * 1024
BIG_VMEM_LIMIT_BYTES = V7X_VMEM_CAPACITY_BYTES - 1024 * 1024

F32 = jnp.float32
BF16 = jnp.bfloat16


def _dot(a, b):
    return jnp.dot(a, b, preferred_element_type=F32)


def _dot_nt(a, b):
    return lax.dot_general(a, b, (((1,), (1,)), ((), ())), preferred_element_type=F32)


def _dot_tn(a, b):
    return lax.dot_general(a, b, (((0,), (0,)), ((), ())), preferred_element_type=F32)


NORM_ROWS = 16


def _norm_modulate_into(h_ref, x_ref, g, shift, scale):
    gain = g * (1.0 + scale)
    for r0 in range(0, x_ref.shape[0], NORM_ROWS):
        rows = slice(r0, r0 + NORM_ROWS)
        x = x_ref[rows, :]
        ms = jnp.mean(x * x, axis=-1, keepdims=True)
        h_ref[rows, :] = (x * lax.rsqrt(ms + EPS) * gain + shift).astype(BF16)


def _ada_kernel(c_ref, w_ref, b_ref, o_ref):
    c = c_ref[...]
    c_act = (c * jax.nn.sigmoid(c)).astype(BF16)
    o_ref[...] = _dot(c_act, w_ref[...].astype(BF16)) + b_ref[...]


def _ada_mod(c, w_ada, b_ada, *, tn):
    bsz, d = c.shape
    n = w_ada.shape[1]
    return pl.pallas_call(
        _ada_kernel,
        grid=(n // tn,),
        in_specs=[
            pl.BlockSpec((bsz, d), lambda j: (0, 0)),
            pl.BlockSpec((d, tn), lambda j: (0, j)),
            pl.BlockSpec((1, tn), lambda j: (0, j)),
        ],
        out_specs=pl.BlockSpec((bsz, tn), lambda j: (0, j)),
        out_shape=jax.ShapeDtypeStruct((bsz, n), F32),
        compiler_params=pltpu.CompilerParams(
            dimension_semantics=("arbitrary",), vmem_limit_bytes=V7X_VMEM_LIMIT_BYTES),
        name="ada_mod",
    )(c, w_ada, b_ada.reshape(1, n))


def _ffn_kernel(x_hbm, mod_ref, g_ref, w1a_ref, w3a_ref, w2a_ref, w1b_ref, w3b_ref, w2b_ref, o_ref,
                x_buf, h_ref, x_sem, *, mod_row):
    i = pl.program_id(0)
    f = pl.program_id(1)
    tm = x_buf.shape[0]

    def x_copy(tile):
        return pltpu.make_async_copy(x_hbm.at[pl.ds(tile * tm, tm), :], x_buf, x_sem)

    def chunk(w1_ref, w3_ref, w2_ref):
        h = h_ref[...]
        a = _dot(h, w1_ref[...])
        b = _dot(h, w3_ref[...])
        p = (a * jax.nn.sigmoid(a) * b).astype(BF16)
        return (0.5 * mod_ref[0, mod_row + 2:mod_row + 3, :]) * _dot(p, w2_ref[...])

    @pl.when(f == 0)
    def _():
        @pl.when(i == 0)
        def _():
            x_copy(0).start()

        x_copy(i).wait()
        _norm_modulate_into(h_ref, x_buf, g_ref[...], mod_ref[0, mod_row:mod_row + 1, :],
                            mod_ref[0, mod_row + 1:mod_row + 2, :])
        o_ref[...] = x_buf[...] + chunk(w1a_ref, w3a_ref, w2a_ref)

    @pl.when(jnp.logical_and(f == 1, i + 1 < pl.num_programs(0)))
    def _():
        x_copy(i + 1).start()

    @pl.when(f != 0)
    def _():
        o_ref[...] += chunk(w1a_ref, w3a_ref, w2a_ref)
        o_ref[...] += chunk(w1b_ref, w3b_ref, w2b_ref)


def _ffn(x2, mod3, g, w1, w3, w2, *, mod_row, seq, tm, tf):
    n, d = x2.shape
    nf = w1.shape[1] // tf
    assert nf % 2 == 1 and nf >= 3, "one chunk at step 0, then pairs"
    tiles_per_seq = seq // tm
    first_of_pair = lambda f: jnp.maximum(2 * f - 1, 0)
    second_of_pair = lambda f: jnp.maximum(2 * f, 2)
    up_a = pl.BlockSpec((d, tf), lambda i, f: (0, first_of_pair(f)))
    up_b = pl.BlockSpec((d, tf), lambda i, f: (0, second_of_pair(f)))
    down_a = pl.BlockSpec((tf, d), lambda i, f: (first_of_pair(f), 0))
    down_b = pl.BlockSpec((tf, d), lambda i, f: (second_of_pair(f), 0))
    return pl.pallas_call(
        functools.partial(_ffn_kernel, mod_row=mod_row),
        grid=(n // tm, (nf + 1) // 2),
        in_specs=[
            pl.BlockSpec(memory_space=pl.ANY),
            pl.BlockSpec((1, N_MOD, d), lambda i, f: (i // tiles_per_seq, 0, 0)),
            pl.BlockSpec((1, d), lambda i, f: (0, 0)),
            up_a, up_a, down_a, up_b, up_b, down_b,
        ],
        out_specs=pl.BlockSpec((tm, d), lambda i, f: (i, 0)),
        out_shape=jax.ShapeDtypeStruct((n, d), F32),
        scratch_shapes=[pltpu.VMEM((tm, d), F32), pltpu.VMEM((tm, d), BF16),
                        pltpu.SemaphoreType.DMA(())],
        compiler_params=pltpu.CompilerParams(
            dimension_semantics=("arbitrary", "arbitrary"),
            vmem_limit_bytes=BIG_VMEM_LIMIT_BYTES),
        name=f"ffn_mod{mod_row}",
    )(x2, mod3, g.reshape(1, d), w1, w3, w2, w1, w3, w2)


def _t5_bucket_table():
    kj = np.arange(2 * BLOCK)[:, None]
    qi = np.arange(BLOCK)[None, :]
    dist = qi + BLOCK - kj
    in_window = (dist >= 0) & (dist < BLOCK)
    n = np.clip(dist, 0, None)
    max_exact = N_BUCKETS // 2
    nf = np.maximum(n, 1).astype(np.float32)
    large = max_exact + (np.log(nf / np.float32(max_exact)) / np.float32(math.log(MAX_DISTANCE / max_exact))
                         * np.float32(N_BUCKETS - max_exact)).astype(np.int32)
    large = np.minimum(large, N_BUCKETS - 1)
    bucket = np.where(n < max_exact, n, large)
    return np.where(in_window, bucket, -1).astype(np.int32)


def _bias_kernel(rb_ref, bucket_ref, o_ref):
    bucket = bucket_ref[...]
    prev_rows = lax.broadcasted_iota(jnp.int32, bucket.shape, 0) < BLOCK
    for head in range(B_HEADS):
        acc = jnp.full(bucket.shape, MASKED, F32)
        for b in range(N_BUCKETS):
            acc = jnp.where(bucket == b, rb_ref[b, head], acc)
        j, hl = divmod(head, GROUP)
        o_ref[0, j, :, hl * BLOCK:(hl + 1) * BLOCK] = acc
        o_ref[1, j, :, hl * BLOCK:(hl + 1) * BLOCK] = jnp.where(prev_rows, MASKED, acc)


def _bias_table(rel_bias):
    return pl.pallas_call(
        _bias_kernel,
        in_specs=[
            pl.BlockSpec(memory_space=pltpu.SMEM),
            pl.BlockSpec((2 * BLOCK, BLOCK), lambda: (0, 0)),
        ],
        out_specs=pl.BlockSpec((2, B_KV_HEADS, 2 * BLOCK, GROUP * BLOCK), lambda: (0, 0, 0, 0)),
        out_shape=jax.ShapeDtypeStruct((2, B_KV_HEADS, 2 * BLOCK, GROUP * BLOCK), F32),
        name="rel_bias_table",
    )(rel_bias, jnp.asarray(_t5_bucket_table()))


W_IN_PREP_ROWS = 256


def _w_in_prep_kernel(w_ref, uv_ref, qkvt_ref):
    w = w_ref[...]
    uv_ref[...] = w[:, :2 * A_WIDTH].astype(BF16)
    qkvt_ref[...] = w[:, 2 * A_WIDTH:].T.astype(BF16)


def _w_in_prep(w_in):
    d, cols = w_in.shape
    rows = min(W_IN_PREP_ROWS, d)
    return pl.pallas_call(
        _w_in_prep_kernel,
        grid=(d // rows,),
        in_specs=[pl.BlockSpec((rows, cols), lambda r: (r, 0))],
        out_specs=[pl.BlockSpec((rows, 2 * A_WIDTH), lambda r: (r, 0)),
                   pl.BlockSpec((QKV_WIDTH, rows), lambda r: (0, r))],
        out_shape=[jax.ShapeDtypeStruct((d, 2 * A_WIDTH), BF16),
                   jax.ShapeDtypeStruct((QKV_WIDTH, d), BF16)],
        compiler_params=pltpu.CompilerParams(dimension_semantics=("arbitrary",)),
        name="w_in_prep",
    )(w_in)


def _gelu(x):
    return 0.5 * x * (1.0 + lax.erf(x * (1.0 / math.sqrt(2.0))))


def _rms_rows(xt, gain):
    ms = jnp.mean(xt * xt, axis=0, keepdims=True)
    return xt * lax.rsqrt(ms + EPS) * gain


def _mix_kernel(sinks_ref, x_ref, mod_ref, g_ref, wuv_ref, wqkvt_ref, wout_ref, sw_ref, sb_ref,
                gv_ref, gq_ref, gk_ref, bias_ref, cast_a_ref, cast_b_ref, cast_c_ref,
                o_ref, cast_a_out, cast_b_out, cast_c_out, h_ref, kprev_ref, vprev_ref, *, tiles_per_seq):
    tm = x_ref.shape[0]
    nblk = tm // BLOCK
    first = pl.program_id(0) % tiles_per_seq == 0

    @pl.when(first)
    def _():
        kprev_ref[...] = jnp.zeros_like(kprev_ref)
        vprev_ref[...] = jnp.zeros_like(vprev_ref)

    _norm_modulate_into(h_ref, x_ref, g_ref[...], mod_ref[0, 3:4, :], mod_ref[0, 4:5, :])
    h = h_ref[...]

    zuv = _dot(h, wuv_ref[...])

    for src, dst in ((cast_a_ref, cast_a_out), (cast_b_ref, cast_b_out), (cast_c_ref, cast_c_out)):
        dst[...] = src[...].astype(BF16)

    u = _gelu(zuv[:, :A_WIDTH])
    v = _gelu(zuv[:, A_WIDTH:])
    row = lax.broadcasted_iota(jnp.int32, (BLOCK, BLOCK), 0)
    col = lax.broadcasted_iota(jnp.int32, (BLOCK, BLOCK), 1)
    causal = row >= col
    ya_heads = []
    for hd in range(A_HEADS):
        cols = slice(hd * A_HEAD_DIM, (hd + 1) * A_HEAD_DIM)
        vh = v[:, cols]
        ms = jnp.mean(vh * vh, axis=-1, keepdims=True)
        vn = (vh * lax.rsqrt(ms + EPS) * gv_ref[:, cols]).astype(BF16)
        w_tril = jnp.where(causal, sw_ref[hd], 0.0).astype(BF16)
        mixed = [_dot(w_tril, vn[c * BLOCK:(c + 1) * BLOCK]) + sb_ref[:, cols] for c in range(nblk)]
        ya_heads.append(u[:, cols] * jnp.concatenate(mixed, axis=0))
    ya = jnp.concatenate(ya_heads, axis=1).astype(BF16)

    gate = mod_ref[0, 5:6, :]
    d_model = o_ref.shape[1]
    slab = max(d_model // (nblk * B_KV_HEADS), BLOCK)
    n_slabs = d_model // slab

    def out_proj_a(unit):
        if unit < n_slabs:
            cols = slice(unit * slab, (unit + 1) * slab)
            o_ref[:, cols] = x_ref[:, cols] + gate[:, cols] * _dot(ya, wout_ref[:A_WIDTH, cols])

    zt = _dot_nt(wqkvt_ref[...], h)
    out_proj_a(0)
    gq = gq_ref[...] * (HEAD_DIM ** -0.5)
    qn = [_rms_rows(zt[hd * HEAD_DIM:(hd + 1) * HEAD_DIM], jnp.tile(gq, (1, nblk))).astype(BF16)
          for hd in range(B_HEADS)]
    kn, vv = [], []
    for j in range(B_KV_HEADS):
        k_rows = slice(B_WIDTH + j * HEAD_DIM, B_WIDTH + (j + 1) * HEAD_DIM)
        v_rows = slice(B_WIDTH + KV_WIDTH + j * HEAD_DIM, B_WIDTH + KV_WIDTH + (j + 1) * HEAD_DIM)
        kn.append(_rms_rows(zt[k_rows], jnp.tile(gk_ref[...], (1, nblk))).astype(BF16))
        vv.append(zt[v_rows].astype(BF16))

    variant = jnp.where(first, 1, 0)
    yb_blocks = []
    for c in range(nblk):
        tok = slice(c * BLOCK, (c + 1) * BLOCK)
        head_rows = [None] * B_HEADS
        for j in range(B_KV_HEADS):
            if c == 0:
                k_prev, v_prev = kprev_ref[j], vprev_ref[j]
                bias = bias_ref[variant, j]
            else:
                prev = slice((c - 1) * BLOCK, c * BLOCK)
                k_prev, v_prev = kn[j][:, prev], vv[j][:, prev]
                bias = bias_ref[0, j]
            k_band = jnp.concatenate([k_prev, kn[j][:, tok]], axis=1)
            v_band = jnp.concatenate([v_prev, vv[j][:, tok]], axis=1)
            q_grp = jnp.concatenate([qn[j * GROUP + hl][:, tok] for hl in range(GROUP)], axis=1)
            s = _dot_tn(k_band, q_grp) + bias
            out_proj_a(c * B_KV_HEADS + j + 1)
            sink = jnp.concatenate(
                [jnp.full((1, BLOCK), sinks_ref[j * GROUP + hl], F32) for hl in range(GROUP)], axis=1)
            m = jnp.maximum(jnp.max(s, axis=0, keepdims=True), sink)
            p = jnp.exp(s - m)
            denom = jnp.sum(p, axis=0, keepdims=True) + jnp.exp(sink - m)
            o_t = _dot(v_band, p.astype(BF16)) / denom
            for hl in range(GROUP):
                head_rows[j * GROUP + hl] = o_t[:, hl * BLOCK:(hl + 1) * BLOCK]
        yb_blocks.append(jnp.concatenate(head_rows, axis=0))
    yb = jnp.concatenate(yb_blocks, axis=1).T

    last = slice((nblk - 1) * BLOCK, nblk * BLOCK)
    for j in range(B_KV_HEADS):
        kprev_ref[j] = kn[j][:, last]
        vprev_ref[j] = vv[j][:, last]

    o_ref[...] += gate * _dot(yb.astype(BF16), wout_ref[A_WIDTH:, :])


def _cast_block_spec(shape, steps):
    rows, cols = shape
    for row_parts in range(steps, 0, -1):
        col_parts = steps // row_parts
        if (row_parts * col_parts == steps and rows % (16 * row_parts) == 0
                and cols % (128 * col_parts) == 0):
            return pl.BlockSpec((rows // row_parts, cols // col_parts),
                                lambda i: (i // col_parts, i % col_parts))
    raise ValueError(f"no {steps}-step tiling of {shape}")


def _mix(x2, mod3, g_mix, w_uv, w_qkvt, w_out, spatial_w, sb_full, gv_row, gq_tab, gk_tab, sinks,
         bias_tab, to_cast, *, seq, tm):
    n, d = x2.shape
    tiles_per_seq = seq // tm
    steps = n // tm
    const2 = lambda i: (0, 0)
    cast_specs = [_cast_block_spec(w.shape, steps) for w in to_cast]
    return pl.pallas_call(
        functools.partial(_mix_kernel, tiles_per_seq=tiles_per_seq),
        grid=(n // tm,),
        in_specs=[
            pl.BlockSpec(memory_space=pltpu.SMEM),
            pl.BlockSpec((tm, d), lambda i: (i, 0)),
            pl.BlockSpec((1, N_MOD, d), lambda i: (i // tiles_per_seq, 0, 0)),
            pl.BlockSpec((1, d), const2),
            pl.BlockSpec(w_uv.shape, const2, pipeline_mode=pl.Buffered(1)),
            pl.BlockSpec(w_qkvt.shape, const2, pipeline_mode=pl.Buffered(1)),
            pl.BlockSpec(w_out.shape, const2, pipeline_mode=pl.Buffered(1)),
            pl.BlockSpec(spatial_w.shape, lambda i: (0, 0, 0), pipeline_mode=pl.Buffered(1)),
            pl.BlockSpec(sb_full.shape, const2, pipeline_mode=pl.Buffered(1)),
            pl.BlockSpec(gv_row.shape, const2),
            pl.BlockSpec(gq_tab.shape, const2),
            pl.BlockSpec(gk_tab.shape, const2),
            pl.BlockSpec(bias_tab.shape, lambda i: (0, 0, 0, 0), pipeline_mode=pl.Buffered(1)),
            *cast_specs,
        ],
        out_specs=[pl.BlockSpec((tm, d), lambda i: (i, 0)), *cast_specs],
        scratch_shapes=[pltpu.VMEM((tm, d), BF16),
                        pltpu.VMEM((B_KV_HEADS, HEAD_DIM, BLOCK), BF16),
                        pltpu.VMEM((B_KV_HEADS, HEAD_DIM, BLOCK), BF16)],
        out_shape=[jax.ShapeDtypeStruct((n, d), F32)]
        + [jax.ShapeDtypeStruct(w.shape, BF16) for w in to_cast],
        compiler_params=pltpu.CompilerParams(
            dimension_semantics=("arbitrary",), vmem_limit_bytes=BIG_VMEM_LIMIT_BYTES),
        name="token_mix",
    )(sinks, x2, mod3, g_mix.reshape(1, d), w_uv, w_qkvt, w_out, spatial_w, sb_full, gv_row,
      gq_tab, gk_tab, bias_tab, *to_cast)


def _tile_sizes(seq, dff):
    tm_ffn = next(t for t in (1024, 512, 256, 128) if seq % t == 0)
    tf = next(t for t in (512, 256, 128) if dff % t == 0)
    tm_mix = next(t for t in (512, 256, 128) if seq % t == 0)
    return tm_ffn, tf, tm_mix


def kernel(x, c, w_ada, b_ada, g_ffn1, w1_ffn1, w3_ffn1, w2_ffn1, g_mix, w_in, spatial_w, spatial_b,
           g_v, g_q, g_k, sinks, rel_bias, w_out, g_ffn2, w1_ffn2, w3_ffn2, w2_ffn2):
    bsz, seq, d = x.shape
    depth = w_ada.shape[0]
    dff = w1_ffn1.shape[-1]
    assert seq % BLOCK == 0 and w_in.shape[-1] == 2 * A_WIDTH + QKV_WIDTH
    tm_ffn, tf, tm_mix = _tile_sizes(seq, dff)
    ada_tn = next(t for t in (1024, 512, 256, 128) if (N_MOD * d) % t == 0)

    bias_tab = _bias_table(rel_bias)
    x2 = x.reshape(bsz * seq, d)
    for l in range(depth):
        mod3 = _ada_mod(c, w_ada[l], b_ada[l], tn=ada_tn).reshape(bsz, N_MOD, d)
        x2 = _ffn(x2, mod3, g_ffn1[l], w1_ffn1[l].astype(BF16), w3_ffn1[l].astype(BF16),
                  w2_ffn1[l].astype(BF16), mod_row=0, seq=seq, tm=tm_ffn, tf=tf)
        w_uv, w_qkvt = _w_in_prep(w_in[l])
        x2, w1b, w3b, w2b = _mix(
            x2, mod3, g_mix[l], w_uv, w_qkvt, w_out[l].astype(BF16),
            spatial_w[l],
            jnp.repeat(spatial_b[l].T, A_HEAD_DIM, axis=1),
            g_v[l].reshape(1, A_WIDTH),
            jnp.broadcast_to(g_q[l][:, None], (HEAD_DIM, BLOCK)),
            jnp.broadcast_to(g_k[l][:, None], (HEAD_DIM, BLOCK)),
            sinks[l], bias_tab,
            (w1_ffn2[l], w3_ffn2[l], w2_ffn2[l]), seq=seq, tm=tm_mix)
        x2 = _ffn(x2, mod3, g_ffn2[l], w1b, w3b, w2b,
                  mod_row=6, seq=seq, tm=tm_ffn, tf=tf)
    return x2.reshape(bsz, seq, d)
```

```python
import functools
import math

import jax
import jax.numpy as jnp
import numpy as np
from jax import lax
from jax.experimental import pallas as pl
from jax.experimental.pallas import tpu as pltpu

BLOCK = 128
A_HEADS = 8
A_HEAD_DIM = 128
A_WIDTH = A_HEADS * A_HEAD_DIM
B_HEADS = 16
B_KV_HEADS = 2
GROUP = B_HEADS // B_KV_HEADS
HEAD_DIM = 64
B_WIDTH = B_HEADS * HEAD_DIM
KV_WIDTH = B_KV_HEADS * HEAD_DIM
QKV_WIDTH = B_WIDTH + 2 * KV_WIDTH
N_BUCKETS = 32
MAX_DISTANCE = 128
N_MOD = 9
EPS = 1e-6
MASKED = -1e30

V7X_VMEM_LIMIT_BYTES = 60000 * 1024
V7X_VMEM_CAPACITY_BYTES = 64 * 1024 * 1024
BIG_VMEM_LIMIT_BYTES = V7X_VMEM_CAPACITY_BYTES - 1024 * 1024

F32 = jnp.float32
BF16 = jnp.bfloat16


def _dot(a, b):
    return jnp.dot(a, b, preferred_element_type=F32)


def _dot_nt(a, b):
    return lax.dot_general(a, b, (((1,), (1,)), ((), ())), preferred_element_type=F32)


def _dot_tn(a, b):
    return lax.dot_general(a, b, (((0,), (0,)), ((), ())), preferred_element_type=F32)


NORM_ROWS = 16


def _norm_modulate_into(h_ref, x_ref, g, shift, scale):
    gain = g * (1.0 + scale)
    for r0 in range(0, x_ref.shape[0], NORM_ROWS):
        rows = slice(r0, r0 + NORM_ROWS)
        x = x_ref[rows, :]
        ms = jnp.mean(x * x, axis=-1, keepdims=True)
        h_ref[rows, :] = (x * lax.rsqrt(ms + EPS) * gain + shift).astype(BF16)


def _ada_kernel(c_ref, w_ref, b_ref, o_ref):
    c = c_ref[...]
    c_act = (c * jax.nn.sigmoid(c)).astype(BF16)
    o_ref[...] = _dot(c_act, w_ref[...].astype(BF16)) + b_ref[...]


def _ada_mod(c, w_ada, b_ada, *, tn):
    bsz, d = c.shape
    n = w_ada.shape[1]
    return pl.pallas_call(
        _ada_kernel,
        grid=(n // tn,),
        in_specs=[
            pl.BlockSpec((bsz, d), lambda j: (0, 0)),
            pl.BlockSpec((d, tn), lambda j: (0, j)),
            pl.BlockSpec((1, tn), lambda j: (0, j)),
        ],
        out_specs=pl.BlockSpec((bsz, tn), lambda j: (0, j)),
        out_shape=jax.ShapeDtypeStruct((bsz, n), F32),
        compiler_params=pltpu.CompilerParams(
            dimension_semantics=("arbitrary",), vmem_limit_bytes=V7X_VMEM_LIMIT_BYTES),
        name="ada_mod",
    )(c, w_ada, b_ada.reshape(1, n))


def _ffn_kernel(x_hbm, mod_ref, g_ref, w1a_ref, w3a_ref, w2a_ref, w1b_ref, w3b_ref, w2b_ref, head_hbm,
                o_ref, x_buf, h_ref, x_sem, head_sem, *, mod_row, has_head):
    i = pl.program_id(0)
    f = pl.program_id(1)
    tm = x_buf.shape[0]
    first = 1 if has_head else 0
    computes = i >= first

    def x_copy(tile):
        return pltpu.make_async_copy(x_hbm.at[pl.ds(tile * tm, tm), :], x_buf, x_sem)

    def chunk(w1_ref, w3_ref, w2_ref):
        h = h_ref[...]
        a = _dot(h, w1_ref[...])
        b = _dot(h, w3_ref[...])
        p = (a * jax.nn.sigmoid(a) * b).astype(BF16)
        return (0.5 * mod_ref[0, mod_row + 2:mod_row + 3, :]) * _dot(p, w2_ref[...])

    if has_head:
        @pl.when(jnp.logical_and(i == 0, f == 0))
        def _():
            head_copy = pltpu.make_async_copy(head_hbm, o_ref, head_sem)
            head_copy.start()
            head_copy.wait()

    @pl.when(jnp.logical_and(f == 0, computes))
    def _():
        if not has_head:
            @pl.when(i == 0)
            def _():
                x_copy(0).start()

        x_copy(i).wait()
        _norm_modulate_into(h_ref, x_buf, g_ref[...], mod_ref[0, mod_row:mod_row + 1, :],
                            mod_ref[0, mod_row + 1:mod_row + 2, :])
        o_ref[...] = x_buf[...] + chunk(w1a_ref, w3a_ref, w2a_ref)

    @pl.when(jnp.logical_and(f == 1, i + 1 < pl.num_programs(0)))
    def _():
        x_copy(i + 1).start()

    @pl.when(jnp.logical_and(f != 0, computes))
    def _():
        o_ref[...] += chunk(w1a_ref, w3a_ref, w2a_ref)
        o_ref[...] += chunk(w1b_ref, w3b_ref, w2b_ref)


FFN_HEAD_TF = 256


def _ffn_head_kernel(x_ref, mod_ref, g_ref, w1_ref, w3_ref, w2_ref, o_ref, w1b_ref, w3b_ref, w2b_ref, h_ref,
                     *, mod_row):
    f = pl.program_id(0)
    w1b_ref[...] = w1_ref[...].astype(BF16)
    w3b_ref[...] = w3_ref[...].astype(BF16)
    w2b_ref[...] = w2_ref[...].astype(BF16)

    def chunk():
        h = h_ref[...]
        a = _dot(h, w1b_ref[...])
        b = _dot(h, w3b_ref[...])
        p = (a * jax.nn.sigmoid(a) * b).astype(BF16)
        return (0.5 * mod_ref[0, mod_row + 2:mod_row + 3, :]) * _dot(p, w2b_ref[...])

    @pl.when(f == 0)
    def _():
        _norm_modulate_into(h_ref, x_ref, g_ref[...], mod_ref[0, mod_row:mod_row + 1, :],
                            mod_ref[0, mod_row + 1:mod_row + 2, :])
        o_ref[...] = x_ref[...] + chunk()

    @pl.when(f != 0)
    def _():
        o_ref[...] += chunk()


def _ffn_head(x2, mod3, g, w1, w3, w2, *, mod_row, tm):
    n, d = x2.shape
    dff = w1.shape[1]
    tf = FFN_HEAD_TF
    up = pl.BlockSpec((d, tf), lambda f: (0, f))
    down = pl.BlockSpec((tf, d), lambda f: (f, 0))
    first_tile = pl.BlockSpec((tm, d), lambda f: (0, 0))
    return pl.pallas_call(
        functools.partial(_ffn_head_kernel, mod_row=mod_row),
        grid=(dff // tf,),
        in_specs=[
            pl.BlockSpec((tm, d), lambda f: (0, 0), pipeline_mode=pl.Buffered(1)),
            pl.BlockSpec((1, N_MOD, d), lambda f: (0, 0, 0)),
            pl.BlockSpec((1, d), lambda f: (0, 0)),
            up, up, down,
        ],
        out_specs=[first_tile, up, up, down],
        out_shape=[jax.ShapeDtypeStruct((tm, d), F32), jax.ShapeDtypeStruct(w1.shape, BF16),
                   jax.ShapeDtypeStruct(w3.shape, BF16), jax.ShapeDtypeStruct(w2.shape, BF16)],
        scratch_shapes=[pltpu.VMEM((tm, d), BF16)],
        compiler_params=pltpu.CompilerParams(
            dimension_semantics=("arbitrary",), vmem_limit_bytes=BIG_VMEM_LIMIT_BYTES),
        name=f"ffn_head_mod{mod_row}",
    )(x2, mod3, g.reshape(1, d), w1, w3, w2)


def _ffn(x2, mod3, g, w1, w3, w2, *, mod_row, seq, tm, tf, head_tile=None):
    n, d = x2.shape
    nf = w1.shape[1] // tf
    assert nf % 2 == 1 and nf >= 3, "one chunk at step 0, then pairs"
    has_head = head_tile is not None
    tiles_per_seq = seq // tm
    first_of_pair = lambda i, f: jnp.where(jnp.logical_and(has_head, i == 0), 0, jnp.maximum(2 * f - 1, 0))
    second_of_pair = lambda i, f: jnp.where(jnp.logical_and(has_head, i == 0), 2, jnp.maximum(2 * f, 2))
    up_a = pl.BlockSpec((d, tf), lambda i, f: (0, first_of_pair(i, f)))
    up_b = pl.BlockSpec((d, tf), lambda i, f: (0, second_of_pair(i, f)))
    down_a = pl.BlockSpec((tf, d), lambda i, f: (first_of_pair(i, f), 0))
    down_b = pl.BlockSpec((tf, d), lambda i, f: (second_of_pair(i, f), 0))
    return pl.pallas_call(
        functools.partial(_ffn_kernel, mod_row=mod_row, has_head=has_head),
        grid=(n // tm, (nf + 1) // 2),
        in_specs=[
            pl.BlockSpec(memory_space=pl.ANY),
            pl.BlockSpec((1, N_MOD, d), lambda i, f: (i // tiles_per_seq, 0, 0)),
            pl.BlockSpec((1, d), lambda i, f: (0, 0)),
            up_a, up_a, down_a, up_b, up_b, down_b,
            pl.BlockSpec(memory_space=pl.ANY),
        ],
        out_specs=pl.BlockSpec((tm, d), lambda i, f: (i, 0)),
        out_shape=jax.ShapeDtypeStruct((n, d), F32),
        scratch_shapes=[pltpu.VMEM((tm, d), F32), pltpu.VMEM((tm, d), BF16),
                        pltpu.SemaphoreType.DMA(()), pltpu.SemaphoreType.DMA(())],
        compiler_params=pltpu.CompilerParams(
            dimension_semantics=("arbitrary", "arbitrary"),
            vmem_limit_bytes=BIG_VMEM_LIMIT_BYTES),
        name=f"ffn_mod{mod_row}",
    )(x2, mod3, g.reshape(1, d), w1, w3, w2, w1, w3, w2, head_tile if has_head else x2)


def _t5_bucket_table():
    kj = np.arange(2 * BLOCK)[:, None]
    qi = np.arange(BLOCK)[None, :]
    dist = qi + BLOCK - kj
    in_window = (dist >= 0) & (dist < BLOCK)
    n = np.clip(dist, 0, None)
    max_exact = N_BUCKETS // 2
    nf = np.maximum(n, 1).astype(np.float32)
    large = max_exact + (np.log(nf / np.float32(max_exact)) / np.float32(math.log(MAX_DISTANCE / max_exact))
                         * np.float32(N_BUCKETS - max_exact)).astype(np.int32)
    large = np.minimum(large, N_BUCKETS - 1)
    bucket = np.where(n < max_exact, n, large)
    return np.where(in_window, bucket, -1).astype(np.int32)


def _bias_kernel(rb_ref, bucket_ref, o_ref):
    bucket = bucket_ref[...]
    prev_rows = lax.broadcasted_iota(jnp.int32, bucket.shape, 0) < BLOCK
    for head in range(B_HEADS):
        acc = jnp.full(bucket.shape, MASKED, F32)
        for b in range(N_BUCKETS):
            acc = jnp.where(bucket == b, rb_ref[b, head], acc)
        j, hl = divmod(head, GROUP)
        o_ref[0, j, :, hl * BLOCK:(hl + 1) * BLOCK] = acc
        o_ref[1, j, :, hl * BLOCK:(hl + 1) * BLOCK] = jnp.where(prev_rows, MASKED, acc)


def _bias_table(rel_bias):
    return pl.pallas_call(
        _bias_kernel,
        in_specs=[
            pl.BlockSpec(memory_space=pltpu.SMEM),
            pl.BlockSpec((2 * BLOCK, BLOCK), lambda: (0, 0)),
        ],
        out_specs=pl.BlockSpec((2, B_KV_HEADS, 2 * BLOCK, GROUP * BLOCK), lambda: (0, 0, 0, 0)),
        out_shape=jax.ShapeDtypeStruct((2, B_KV_HEADS, 2 * BLOCK, GROUP * BLOCK), F32),
        name="rel_bias_table",
    )(rel_bias, jnp.asarray(_t5_bucket_table()))


W_IN_PREP_ROWS = 256


def _w_in_prep_kernel(w_ref, uv_ref, qkvt_ref):
    w = w_ref[...]
    uv_ref[...] = w[:, :2 * A_WIDTH].astype(BF16)
    qkvt_ref[...] = w[:, 2 * A_WIDTH:].T.astype(BF16)


def _w_in_prep(w_in):
    d, cols = w_in.shape
    rows = min(W_IN_PREP_ROWS, d)
    return pl.pallas_call(
        _w_in_prep_kernel,
        grid=(d // rows,),
        in_specs=[pl.BlockSpec((rows, cols), lambda r: (r, 0))],
        out_specs=[pl.BlockSpec((rows, 2 * A_WIDTH), lambda r: (r, 0)),
                   pl.BlockSpec((QKV_WIDTH, rows), lambda r: (0, r))],
        out_shape=[jax.ShapeDtypeStruct((d, 2 * A_WIDTH), BF16),
                   jax.ShapeDtypeStruct((QKV_WIDTH, d), BF16)],
        compiler_params=pltpu.CompilerParams(dimension_semantics=("arbitrary",)),
        name="w_in_prep",
    )(w_in)


def _gelu(x):
    return 0.5 * x * (1.0 + lax.erf(x * (1.0 / math.sqrt(2.0))))


def _rms_rows(xt, gain):
    ms = jnp.mean(xt * xt, axis=0, keepdims=True)
    return xt * lax.rsqrt(ms + EPS) * gain


def _mix_kernel(sinks_ref, x_ref, mod_ref, g_ref, wuv_ref, wqkvt_ref, wout_ref, sw_ref, sb_ref,
                gv_ref, gq_ref, gk_ref, bias_ref, cast_a_ref, cast_b_ref, cast_c_ref,
                o_ref, cast_a_out, cast_b_out, cast_c_out, h_ref, kprev_ref, vprev_ref, *, tiles_per_seq):
    tm = x_ref.shape[0]
    nblk = tm // BLOCK
    first = pl.program_id(0) % tiles_per_seq == 0

    @pl.when(first)
    def _():
        kprev_ref[...] = jnp.zeros_like(kprev_ref)
        vprev_ref[...] = jnp.zeros_like(vprev_ref)

    _norm_modulate_into(h_ref, x_ref, g_ref[...], mod_ref[0, 3:4, :], mod_ref[0, 4:5, :])
    h = h_ref[...]

    zuv = _dot(h, wuv_ref[...])

    for src, dst in ((cast_a_ref, cast_a_out), (cast_b_ref, cast_b_out), (cast_c_ref, cast_c_out)):
        dst[...] = src[...].astype(BF16)

    u = _gelu(zuv[:, :A_WIDTH])
    v = _gelu(zuv[:, A_WIDTH:])
    row = lax.broadcasted_iota(jnp.int32, (BLOCK, BLOCK), 0)
    col = lax.broadcasted_iota(jnp.int32, (BLOCK, BLOCK), 1)
    causal = row >= col
    ya_heads = []
    for hd in range(A_HEADS):
        cols = slice(hd * A_HEAD_DIM, (hd + 1) * A_HEAD_DIM)
        vh = v[:, cols]
        ms = jnp.mean(vh * vh, axis=-1, keepdims=True)
        vn = (vh * lax.rsqrt(ms + EPS) * gv_ref[:, cols]).astype(BF16)
        w_tril = jnp.where(causal, sw_ref[hd], 0.0).astype(BF16)
        mixed = [_dot(w_tril, vn[c * BLOCK:(c + 1) * BLOCK]) + sb_ref[:, cols] for c in range(nblk)]
        ya_heads.append(u[:, cols] * jnp.concatenate(mixed, axis=0))
    ya = jnp.concatenate(ya_heads, axis=1).astype(BF16)

    gate = mod_ref[0, 5:6, :]
    d_model = o_ref.shape[1]
    slab = max(d_model // (nblk * B_KV_HEADS), BLOCK)
    n_slabs = d_model // slab

    def out_proj_a(unit):
        if unit < n_slabs:
            cols = slice(unit * slab, (unit + 1) * slab)
            o_ref[:, cols] = x_ref[:, cols] + gate[:, cols] * _dot(ya, wout_ref[:A_WIDTH, cols])

    zt = _dot_nt(wqkvt_ref[...], h)
    out_proj_a(0)
    gq = gq_ref[...] * (HEAD_DIM ** -0.5)
    qn = [_rms_rows(zt[hd * HEAD_DIM:(hd + 1) * HEAD_DIM], jnp.tile(gq, (1, nblk))).astype(BF16)
          for hd in range(B_HEADS)]
    kn, vv = [], []
    for j in range(B_KV_HEADS):
        k_rows = slice(B_WIDTH + j * HEAD_DIM, B_WIDTH + (j + 1) * HEAD_DIM)
        v_rows = slice(B_WIDTH + KV_WIDTH + j * HEAD_DIM, B_WIDTH + KV_WIDTH + (j + 1) * HEAD_DIM)
        kn.append(_rms_rows(zt[k_rows], jnp.tile(gk_ref[...], (1, nblk))).astype(BF16))
        vv.append(zt[v_rows].astype(BF16))

    variant = jnp.where(first, 1, 0)
    yb_blocks = []
    for c in range(nblk):
        tok = slice(c * BLOCK, (c + 1) * BLOCK)
        head_rows = [None] * B_HEADS
        for j in range(B_KV_HEADS):
            if c == 0:
                k_prev, v_prev = kprev_ref[j], vprev_ref[j]
                bias = bias_ref[variant, j]
            else:
                prev = slice((c - 1) * BLOCK, c * BLOCK)
                k_prev, v_prev = kn[j][:, prev], vv[j][:, prev]
                bias = bias_ref[0, j]
            k_band = jnp.concatenate([k_prev, kn[j][:, tok]], axis=1)
            v_band = jnp.concatenate([v_prev, vv[j][:, tok]], axis=1)
            q_grp = jnp.concatenate([qn[j * GROUP + hl][:, tok] for hl in range(GROUP)], axis=1)
            s = _dot_tn(k_band, q_grp) + bias
            out_proj_a(c * B_KV_HEADS + j + 1)
            sink = jnp.concatenate(
                [jnp.full((1, BLOCK), sinks_ref[j * GROUP + hl], F32) for hl in range(GROUP)], axis=1)
            m = jnp.maximum(jnp.max(s, axis=0, keepdims=True), sink)
            p = jnp.exp(s - m)
            denom = jnp.sum(p, axis=0, keepdims=True) + jnp.exp(sink - m)
            o_t = _dot(v_band, p.astype(BF16)) / denom
            for hl in range(GROUP):
                head_rows[j * GROUP + hl] = o_t[:, hl * BLOCK:(hl + 1) * BLOCK]
        yb_blocks.append(jnp.concatenate(head_rows, axis=0))
    yb = jnp.concatenate(yb_blocks, axis=1).T

    last = slice((nblk - 1) * BLOCK, nblk * BLOCK)
    for j in range(B_KV_HEADS):
        kprev_ref[j] = kn[j][:, last]
        vprev_ref[j] = vv[j][:, last]

    o_ref[...] += gate * _dot(yb.astype(BF16), wout_ref[A_WIDTH:, :])


def _cast_block_spec(shape, steps):
    rows, cols = shape
    for row_parts in range(steps, 0, -1):
        col_parts = steps // row_parts
        if (row_parts * col_parts == steps and rows % (16 * row_parts) == 0
                and cols % (128 * col_parts) == 0):
            return pl.BlockSpec((rows // row_parts, cols // col_parts),
                                lambda i: (i // col_parts, i % col_parts))
    raise ValueError(f"no {steps}-step tiling of {shape}")


def _mix(x2, mod3, g_mix, w_uv, w_qkvt, w_out, spatial_w, sb_full, gv_row, gq_tab, gk_tab, sinks,
         bias_tab, to_cast, *, seq, tm):
    n, d = x2.shape
    tiles_per_seq = seq // tm
    steps = n // tm
    const2 = lambda i: (0, 0)
    cast_specs = [_cast_block_spec(w.shape, steps) for w in to_cast]
    return pl.pallas_call(
        functools.partial(_mix_kernel, tiles_per_seq=tiles_per_seq),
        grid=(n // tm,),
        in_specs=[
            pl.BlockSpec(memory_space=pltpu.SMEM),
            pl.BlockSpec((tm, d), lambda i: (i, 0)),
            pl.BlockSpec((1, N_MOD, d), lambda i: (i // tiles_per_seq, 0, 0)),
            pl.BlockSpec((1, d), const2),
            pl.BlockSpec(w_uv.shape, const2, pipeline_mode=pl.Buffered(1)),
            pl.BlockSpec(w_qkvt.shape, const2, pipeline_mode=pl.Buffered(1)),
            pl.BlockSpec(w_out.shape, const2, pipeline_mode=pl.Buffered(1)),
            pl.BlockSpec(spatial_w.shape, lambda i: (0, 0, 0), pipeline_mode=pl.Buffered(1)),
            pl.BlockSpec(sb_full.shape, const2, pipeline_mode=pl.Buffered(1)),
            pl.BlockSpec(gv_row.shape, const2),
            pl.BlockSpec(gq_tab.shape, const2),
            pl.BlockSpec(gk_tab.shape, const2),
            pl.BlockSpec(bias_tab.shape, lambda i: (0, 0, 0, 0), pipeline_mode=pl.Buffered(1)),
            *cast_specs,
        ],
        out_specs=[pl.BlockSpec((tm, d), lambda i: (i, 0)), *cast_specs],
        scratch_shapes=[pltpu.VMEM((tm, d), BF16),
                        pltpu.VMEM((B_KV_HEADS, HEAD_DIM, BLOCK), BF16),
                        pltpu.VMEM((B_KV_HEADS, HEAD_DIM, BLOCK), BF16)],
        out_shape=[jax.ShapeDtypeStruct((n, d), F32)]
        + [jax.ShapeDtypeStruct(w.shape, BF16) for w in to_cast],
        compiler_params=pltpu.CompilerParams(
            dimension_semantics=("arbitrary",), vmem_limit_bytes=BIG_VMEM_LIMIT_BYTES),
        name="token_mix",
    )(sinks, x2, mod3, g_mix.reshape(1, d), w_uv, w_qkvt, w_out, spatial_w, sb_full, gv_row,
      gq_tab, gk_tab, bias_tab, *to_cast)


def _tile_sizes(seq, dff):
    tm_ffn = next(t for t in (1024, 512, 256, 128) if seq % t == 0)
    tf = next(t for t in (512, 256, 128) if dff % t == 0)
    tm_mix = next(t for t in (512, 256, 128) if seq % t == 0)
    return tm_ffn, tf, tm_mix


def kernel(x, c, w_ada, b_ada, g_ffn1, w1_ffn1, w3_ffn1, w2_ffn1, g_mix, w_in, spatial_w, spatial_b,
           g_v, g_q, g_k, sinks, rel_bias, w_out, g_ffn2, w1_ffn2, w3_ffn2, w2_ffn2):
    bsz, seq, d = x.shape
    depth = w_ada.shape[0]
    dff = w1_ffn1.shape[-1]
    assert seq % BLOCK == 0 and w_in.shape[-1] == 2 * A_WIDTH + QKV_WIDTH
    tm_ffn, tf, tm_mix = _tile_sizes(seq, dff)
    ada_tn = next(t for t in (1024, 512, 256, 128) if (N_MOD * d) % t == 0)

    bias_tab = _bias_table(rel_bias)
    x2 = x.reshape(bsz * seq, d)
    for l in range(depth):
        mod3 = _ada_mod(c, w_ada[l], b_ada[l], tn=ada_tn).reshape(bsz, N_MOD, d)
        head_tile, w1a, w3a, w2a = _ffn_head(x2, mod3, g_ffn1[l], w1_ffn1[l], w3_ffn1[l], w2_ffn1[l],
                                             mod_row=0, tm=tm_ffn)
        x2 = _ffn(x2, mod3, g_ffn1[l], w1a, w3a, w2a, mod_row=0, seq=seq, tm=tm_ffn, tf=tf,
                  head_tile=head_tile)
        w_uv, w_qkvt = _w_in_prep(w_in[l])
        x2, w1b, w3b, w2b = _mix(
            x2, mod3, g_mix[l], w_uv, w_qkvt, w_out[l].astype(BF16),
            spatial_w[l],
            jnp.repeat(spatial_b[l].T, A_HEAD_DIM, axis=1),
            g_v[l].reshape(1, A_WIDTH),
            jnp.broadcast_to(g_q[l][:, None], (HEAD_DIM, BLOCK)),
            jnp.broadcast_to(g_k[l][:, None], (HEAD_DIM, BLOCK)),
            sinks[l], bias_tab,
            (w1_ffn2[l], w3_ffn2[l], w2_ffn2[l]), seq=seq, tm=tm_mix)
        x2 = _ffn(x2, mod3, g_ffn2[l], w1b, w3b, w2b,
                  mod_row=6, seq=seq, tm=tm_ffn, tf=tf)
    return x2.reshape(bsz, seq, d)
```

```python
import functools
import math

import jax
import jax.numpy as jnp
import numpy as np
from jax import lax
from jax.experimental import pallas as pl
from jax.experimental.pallas import tpu as pltpu

BLOCK = 128
A_HEADS = 8
A_HEAD_DIM = 128
A_WIDTH = A_HEADS * A_HEAD_DIM
B_HEADS = 16
B_KV_HEADS = 2
GROUP = B_HEADS // B_KV_HEADS
HEAD_DIM = 64
B_WIDTH = B_HEADS * HEAD_DIM
KV_WIDTH = B_KV_HEADS * HEAD_DIM
QKV_WIDTH = B_WIDTH + 2 * KV_WIDTH
N_BUCKETS = 32
MAX_DISTANCE = 128
N_MOD = 9
EPS = 1e-6
MASKED = float("-inf")

V7X_VMEM_LIMIT_BYTES = 60000 * 1024
V7X_VMEM_CAPACITY_BYTES = 64 * 1024 * 1024
BIG_VMEM_LIMIT_BYTES = V7X_VMEM_CAPACITY_BYTES - 1024 * 1024

F32 = jnp.float32
BF16 = jnp.bfloat16


def _dot(a, b):
    return jnp.dot(a, b, preferred_element_type=F32)


def _dot_nt(a, b):
    return lax.dot_general(a, b, (((1,), (1,)), ((), ())), preferred_element_type=F32)


def _dot_tn(a, b):
    return lax.dot_general(a, b, (((0,), (0,)), ((), ())), preferred_element_type=F32)


NORM_ROWS = 16


def _norm_modulate_into(h_ref, x_ref, g, shift, scale):
    gain = g * (1.0 + scale)
    for r0 in range(0, x_ref.shape[0], NORM_ROWS):
        rows = slice(r0, r0 + NORM_ROWS)
        x = x_ref[rows, :]
        ms = jnp.mean(x * x, axis=-1, keepdims=True)
        h_ref[rows, :] = (x * lax.rsqrt(ms + EPS) * gain + shift).astype(BF16)


def _ada_kernel(c_ref, w_ref, b_ref, o_ref):
    c = c_ref[...]
    c_act = (c * jax.nn.sigmoid(c)).astype(BF16)
    o_ref[...] = _dot(c_act, w_ref[...].astype(BF16)) + b_ref[...]


def _ada_mod(c, w_ada, b_ada, *, tn):
    bsz, d = c.shape
    n = w_ada.shape[1]
    return pl.pallas_call(
        _ada_kernel,
        grid=(n // tn,),
        in_specs=[
            pl.BlockSpec((bsz, d), lambda j: (0, 0)),
            pl.BlockSpec((d, tn), lambda j: (0, j)),
            pl.BlockSpec((1, tn), lambda j: (0, j)),
        ],
        out_specs=pl.BlockSpec((bsz, tn), lambda j: (0, j)),
        out_shape=jax.ShapeDtypeStruct((bsz, n), F32),
        compiler_params=pltpu.CompilerParams(
            dimension_semantics=("arbitrary",), vmem_limit_bytes=V7X_VMEM_LIMIT_BYTES),
        name="ada_mod",
    )(c, w_ada, b_ada.reshape(1, n))


def _ffn_kernel(x_hbm, mod_ref, g_ref, w1a_ref, w3a_ref, w2a_ref, w1b_ref, w3b_ref, w2b_ref, head_hbm,
                o_ref, x_buf, h_ref, x_sem, head_sem, *, mod_row, has_head):
    i = pl.program_id(0)
    f = pl.program_id(1)
    tm = x_buf.shape[0]
    first = 1 if has_head else 0
    computes = i >= first

    def x_copy(tile):
        return pltpu.make_async_copy(x_hbm.at[pl.ds(tile * tm, tm), :], x_buf, x_sem)

    def chunk(w1_ref, w3_ref, w2_ref):
        h = h_ref[...]
        a = _dot(h, w1_ref[...])
        b = _dot(h, w3_ref[...])
        p = (a * jax.nn.sigmoid(a) * b).astype(BF16)
        return (0.5 * mod_ref[0, mod_row + 2:mod_row + 3, :]) * _dot(p, w2_ref[...])

    if has_head:
        @pl.when(jnp.logical_and(i == 0, f == 0))
        def _():
            head_copy = pltpu.make_async_copy(head_hbm, o_ref, head_sem)
            head_copy.start()
            head_copy.wait()

    @pl.when(jnp.logical_and(f == 0, computes))
    def _():
        if not has_head:
            @pl.when(i == 0)
            def _():
                x_copy(0).start()

        x_copy(i).wait()
        _norm_modulate_into(h_ref, x_buf, g_ref[...], mod_ref[0, mod_row:mod_row + 1, :],
                            mod_ref[0, mod_row + 1:mod_row + 2, :])
        o_ref[...] = x_buf[...] + chunk(w1a_ref, w3a_ref, w2a_ref)

    @pl.when(jnp.logical_and(f == 1, i + 1 < pl.num_programs(0)))
    def _():
        x_copy(i + 1).start()

    @pl.when(jnp.logical_and(f != 0, computes))
    def _():
        o_ref[...] += chunk(w1a_ref, w3a_ref, w2a_ref)
        o_ref[...] += chunk(w1b_ref, w3b_ref, w2b_ref)


FFN_HEAD_TF = 256


def _ffn_head_kernel(x_ref, mod_ref, g_ref, w1_ref, w3_ref, w2_ref, o_ref, w1b_ref, w3b_ref, w2b_ref, h_ref,
                     *, mod_row):
    f = pl.program_id(0)
    w1b_ref[...] = w1_ref[...].astype(BF16)
    w3b_ref[...] = w3_ref[...].astype(BF16)
    w2b_ref[...] = w2_ref[...].astype(BF16)

    def chunk():
        h = h_ref[...]
        a = _dot(h, w1b_ref[...])
        b = _dot(h, w3b_ref[...])
        p = (a * jax.nn.sigmoid(a) * b).astype(BF16)
        return (0.5 * mod_ref[0, mod_row + 2:mod_row + 3, :]) * _dot(p, w2b_ref[...])

    @pl.when(f == 0)
    def _():
        _norm_modulate_into(h_ref, x_ref, g_ref[...], mod_ref[0, mod_row:mod_row + 1, :],
                            mod_ref[0, mod_row + 1:mod_row + 2, :])
        o_ref[...] = x_ref[...] + chunk()

    @pl.when(f != 0)
    def _():
        o_ref[...] += chunk()


def _ffn_head(x2, mod3, g, w1, w3, w2, *, mod_row, tm):
    n, d = x2.shape
    dff = w1.shape[1]
    tf = FFN_HEAD_TF
    up = pl.BlockSpec((d, tf), lambda f: (0, f))
    down = pl.BlockSpec((tf, d), lambda f: (f, 0))
    first_tile = pl.BlockSpec((tm, d), lambda f: (0, 0))
    return pl.pallas_call(
        functools.partial(_ffn_head_kernel, mod_row=mod_row),
        grid=(dff // tf,),
        in_specs=[
            pl.BlockSpec((tm, d), lambda f: (0, 0), pipeline_mode=pl.Buffered(1)),
            pl.BlockSpec((1, N_MOD, d), lambda f: (0, 0, 0)),
            pl.BlockSpec((1, d), lambda f: (0, 0)),
            up, up, down,
        ],
        out_specs=[first_tile, up, up, down],
        out_shape=[jax.ShapeDtypeStruct((tm, d), F32), jax.ShapeDtypeStruct(w1.shape, BF16),
                   jax.ShapeDtypeStruct(w3.shape, BF16), jax.ShapeDtypeStruct(w2.shape, BF16)],
        scratch_shapes=[pltpu.VMEM((tm, d), BF16)],
        compiler_params=pltpu.CompilerParams(
            dimension_semantics=("arbitrary",), vmem_limit_bytes=BIG_VMEM_LIMIT_BYTES),
        name=f"ffn_head_mod{mod_row}",
    )(x2, mod3, g.reshape(1, d), w1, w3, w2)


def _ffn(x2, mod3, g, w1, w3, w2, *, mod_row, seq, tm, tf, head_tile=None):
    n, d = x2.shape
    nf = w1.shape[1] // tf
    assert nf % 2 == 1 and nf >= 3, "one chunk at step 0, then pairs"
    has_head = head_tile is not None
    tiles_per_seq = seq // tm
    first_of_pair = lambda i, f: jnp.where(jnp.logical_and(has_head, i == 0), 0, jnp.maximum(2 * f - 1, 0))
    second_of_pair = lambda i, f: jnp.where(jnp.logical_and(has_head, i == 0), 2, jnp.maximum(2 * f, 2))
    up_a = pl.BlockSpec((d, tf), lambda i, f: (0, first_of_pair(i, f)))
    up_b = pl.BlockSpec((d, tf), lambda i, f: (0, second_of_pair(i, f)))
    down_a = pl.BlockSpec((tf, d), lambda i, f: (first_of_pair(i, f), 0))
    down_b = pl.BlockSpec((tf, d), lambda i, f: (second_of_pair(i, f), 0))
    return pl.pallas_call(
        functools.partial(_ffn_kernel, mod_row=mod_row, has_head=has_head),
        grid=(n // tm, (nf + 1) // 2),
        in_specs=[
            pl.BlockSpec(memory_space=pl.ANY),
            pl.BlockSpec((1, N_MOD, d), lambda i, f: (i // tiles_per_seq, 0, 0)),
            pl.BlockSpec((1, d), lambda i, f: (0, 0)),
            up_a, up_a, down_a, up_b, up_b, down_b,
            pl.BlockSpec(memory_space=pl.ANY),
        ],
        out_specs=pl.BlockSpec((tm, d), lambda i, f: (i, 0)),
        out_shape=jax.ShapeDtypeStruct((n, d), F32),
        scratch_shapes=[pltpu.VMEM((tm, d), F32), pltpu.VMEM((tm, d), BF16),
                        pltpu.SemaphoreType.DMA(()), pltpu.SemaphoreType.DMA(())],
        compiler_params=pltpu.CompilerParams(
            dimension_semantics=("arbitrary", "arbitrary"),
            vmem_limit_bytes=BIG_VMEM_LIMIT_BYTES),
        name=f"ffn_mod{mod_row}",
    )(x2, mod3, g.reshape(1, d), w1, w3, w2, w1, w3, w2, head_tile if has_head else x2)


def _t5_bucket_table():
    kj = np.arange(2 * BLOCK)[:, None]
    qi = np.arange(BLOCK)[None, :]
    dist = qi + BLOCK - kj
    in_window = (dist >= 0) & (dist < BLOCK)
    n = np.clip(dist, 0, None)
    max_exact = N_BUCKETS // 2
    nf = np.maximum(n, 1).astype(np.float32)
    large = max_exact + (np.log(nf / np.float32(max_exact)) / np.float32(math.log(MAX_DISTANCE / max_exact))
                         * np.float32(N_BUCKETS - max_exact)).astype(np.int32)
    large = np.minimum(large, N_BUCKETS - 1)
    bucket = np.where(n < max_exact, n, large)
    return np.where(in_window, bucket, -1).astype(np.int32)


def _bias_kernel(rb_ref, bucket_ref, o_ref):
    bucket = bucket_ref[...]
    prev_rows = lax.broadcasted_iota(jnp.int32, bucket.shape, 0) < BLOCK
    for head in range(B_HEADS):
        acc = jnp.full(bucket.shape, MASKED, F32)
        for b in range(N_BUCKETS):
            acc = jnp.where(bucket == b, rb_ref[b, head], acc)
        j, hl = divmod(head, GROUP)
        o_ref[0, j, :, hl * BLOCK:(hl + 1) * BLOCK] = acc
        o_ref[1, j, :, hl * BLOCK:(hl + 1) * BLOCK] = jnp.where(prev_rows, MASKED, acc)


def _bias_table(rel_bias):
    return pl.pallas_call(
        _bias_kernel,
        in_specs=[
            pl.BlockSpec(memory_space=pltpu.SMEM),
            pl.BlockSpec((2 * BLOCK, BLOCK), lambda: (0, 0)),
        ],
        out_specs=pl.BlockSpec((2, B_KV_HEADS, 2 * BLOCK, GROUP * BLOCK), lambda: (0, 0, 0, 0)),
        out_shape=jax.ShapeDtypeStruct((2, B_KV_HEADS, 2 * BLOCK, GROUP * BLOCK), F32),
        name="rel_bias_table",
    )(rel_bias, jnp.asarray(_t5_bucket_table()))


W_IN_PREP_ROWS = 256


def _w_in_prep_kernel(w_ref, uv_ref, qkvt_ref):
    w = w_ref[...]
    uv_ref[...] = w[:, :2 * A_WIDTH].astype(BF16)
    qkvt_ref[...] = w[:, 2 * A_WIDTH:].T.astype(BF16)


def _w_in_prep(w_in):
    d, cols = w_in.shape
    rows = min(W_IN_PREP_ROWS, d)
    return pl.pallas_call(
        _w_in_prep_kernel,
        grid=(d // rows,),
        in_specs=[pl.BlockSpec((rows, cols), lambda r: (r, 0))],
        out_specs=[pl.BlockSpec((rows, 2 * A_WIDTH), lambda r: (r, 0)),
                   pl.BlockSpec((QKV_WIDTH, rows), lambda r: (0, r))],
        out_shape=[jax.ShapeDtypeStruct((d, 2 * A_WIDTH), BF16),
                   jax.ShapeDtypeStruct((QKV_WIDTH, d), BF16)],
        compiler_params=pltpu.CompilerParams(dimension_semantics=("arbitrary",)),
        name="w_in_prep",
    )(w_in)


def _gelu(x):
    return 0.5 * x * (1.0 + lax.erf(x * (1.0 / math.sqrt(2.0))))


def _rms_rows(xt, gain):
    ms = jnp.mean(xt * xt, axis=0, keepdims=True)
    return xt * lax.rsqrt(ms + EPS) * gain


def _mix_kernel(sinks_ref, x_ref, mod_ref, g_ref, wuv_ref, wqkvt_ref, wout_ref, sw_ref, sb_ref,
                gv_ref, gq_ref, gk_ref, bias_ref, cast_a_ref, cast_b_ref, cast_c_ref,
                o_ref, cast_a_out, cast_b_out, cast_c_out, h_ref, kprev_ref, vprev_ref, *, tiles_per_seq):
    tm = x_ref.shape[0]
    nblk = tm // BLOCK
    first = pl.program_id(0) % tiles_per_seq == 0

    @pl.when(first)
    def _():
        kprev_ref[...] = jnp.zeros_like(kprev_ref)
        vprev_ref[...] = jnp.zeros_like(vprev_ref)

    _norm_modulate_into(h_ref, x_ref, g_ref[...], mod_ref[0, 3:4, :], mod_ref[0, 4:5, :])
    h = h_ref[...]

    zuv = _dot(h, wuv_ref[...])

    for src, dst in ((cast_a_ref, cast_a_out), (cast_b_ref, cast_b_out), (cast_c_ref, cast_c_out)):
        dst[...] = src[...].astype(BF16)

    u = _gelu(zuv[:, :A_WIDTH])
    v = _gelu(zuv[:, A_WIDTH:])
    row = lax.broadcasted_iota(jnp.int32, (BLOCK, BLOCK), 0)
    col = lax.broadcasted_iota(jnp.int32, (BLOCK, BLOCK), 1)
    causal = row >= col
    ya_heads = []
    for hd in range(A_HEADS):
        cols = slice(hd * A_HEAD_DIM, (hd + 1) * A_HEAD_DIM)
        vh = v[:, cols]
        ms = jnp.mean(vh * vh, axis=-1, keepdims=True)
        vn = (vh * lax.rsqrt(ms + EPS) * gv_ref[:, cols]).astype(BF16)
        w_tril = jnp.where(causal, sw_ref[hd], 0.0).astype(BF16)
        mixed = [_dot(w_tril, vn[c * BLOCK:(c + 1) * BLOCK]) + sb_ref[:, cols] for c in range(nblk)]
        ya_heads.append(u[:, cols] * jnp.concatenate(mixed, axis=0))
    ya = jnp.concatenate(ya_heads, axis=1).astype(BF16)

    gate = mod_ref[0, 5:6, :]
    d_model = o_ref.shape[1]
    slab = max(d_model // (nblk * B_KV_HEADS), BLOCK)
    n_slabs = d_model // slab

    def out_proj_a(unit):
        if unit < n_slabs:
            cols = slice(unit * slab, (unit + 1) * slab)
            o_ref[:, cols] = x_ref[:, cols] + gate[:, cols] * _dot(ya, wout_ref[:A_WIDTH, cols])

    zt = _dot_nt(wqkvt_ref[...], h)
    out_proj_a(0)
    gq = gq_ref[...] * (HEAD_DIM ** -0.5)
    qn = [_rms_rows(zt[hd * HEAD_DIM:(hd + 1) * HEAD_DIM], jnp.tile(gq, (1, nblk))).astype(BF16)
          for hd in range(B_HEADS)]
    kn, vv = [], []
    for j in range(B_KV_HEADS):
        k_rows = slice(B_WIDTH + j * HEAD_DIM, B_WIDTH + (j + 1) * HEAD_DIM)
        v_rows = slice(B_WIDTH + KV_WIDTH + j * HEAD_DIM, B_WIDTH + KV_WIDTH + (j + 1) * HEAD_DIM)
        kn.append(_rms_rows(zt[k_rows], jnp.tile(gk_ref[...], (1, nblk))).astype(BF16))
        vv.append(zt[v_rows].astype(BF16))

    variant = jnp.where(first, 1, 0)
    yb_blocks = []
    for c in range(nblk):
        tok = slice(c * BLOCK, (c + 1) * BLOCK)
        head_rows = [None] * B_HEADS
        for j in range(B_KV_HEADS):
            if c == 0:
                k_prev, v_prev = kprev_ref[j], vprev_ref[j]
                bias = bias_ref[variant, j]
            else:
                prev = slice((c - 1) * BLOCK, c * BLOCK)
                k_prev, v_prev = kn[j][:, prev], vv[j][:, prev]
                bias = bias_ref[0, j]
            k_band = jnp.concatenate([k_prev, kn[j][:, tok]], axis=1)
            v_band = jnp.concatenate([v_prev, vv[j][:, tok]], axis=1)
            q_grp = jnp.concatenate([qn[j * GROUP + hl][:, tok] for hl in range(GROUP)], axis=1)
            s = _dot_tn(k_band, q_grp) + bias
            out_proj_a(c * B_KV_HEADS + j + 1)
            sink = jnp.concatenate(
                [jnp.full((1, BLOCK), sinks_ref[j * GROUP + hl], F32) for hl in range(GROUP)], axis=1)
            m = jnp.maximum(jnp.max(s, axis=0, keepdims=True), sink)
            p = jnp.exp(s - m)
            denom = jnp.sum(p, axis=0, keepdims=True) + jnp.exp(sink - m)
            o_t = _dot(v_band, p.astype(BF16)) / denom
            for hl in range(GROUP):
                head_rows[j * GROUP + hl] = o_t[:, hl * BLOCK:(hl + 1) * BLOCK]
        yb_blocks.append(jnp.concatenate(head_rows, axis=0))
    yb = jnp.concatenate(yb_blocks, axis=1).T

    last = slice((nblk - 1) * BLOCK, nblk * BLOCK)
    for j in range(B_KV_HEADS):
        kprev_ref[j] = kn[j][:, last]
        vprev_ref[j] = vv[j][:, last]

    o_ref[...] += gate * _dot(yb.astype(BF16), wout_ref[A_WIDTH:, :])


def _cast_block_spec(shape, steps):
    rows, cols = shape
    for row_parts in range(steps, 0, -1):
        col_parts = steps // row_parts
        if (row_parts * col_parts == steps and rows % (16 * row_parts) == 0
                and cols % (128 * col_parts) == 0):
            return pl.BlockSpec((rows // row_parts, cols // col_parts),
                                lambda i: (i // col_parts, i % col_parts))
    raise ValueError(f"no {steps}-step tiling of {shape}")


def _mix(x2, mod3, g_mix, w_uv, w_qkvt, w_out, spatial_w, sb_full, gv_row, gq_tab, gk_tab, sinks,
         bias_tab, to_cast, *, seq, tm):
    n, d = x2.shape
    tiles_per_seq = seq // tm
    steps = n // tm
    const2 = lambda i: (0, 0)
    cast_specs = [_cast_block_spec(w.shape, steps) for w in to_cast]
    return pl.pallas_call(
        functools.partial(_mix_kernel, tiles_per_seq=tiles_per_seq),
        grid=(n // tm,),
        in_specs=[
            pl.BlockSpec(memory_space=pltpu.SMEM),
            pl.BlockSpec((tm, d), lambda i: (i, 0)),
            pl.BlockSpec((1, N_MOD, d), lambda i: (i // tiles_per_seq, 0, 0)),
            pl.BlockSpec((1, d), const2),
            pl.BlockSpec(w_uv.shape, const2, pipeline_mode=pl.Buffered(1)),
            pl.BlockSpec(w_qkvt.shape, const2, pipeline_mode=pl.Buffered(1)),
            pl.BlockSpec(w_out.shape, const2, pipeline_mode=pl.Buffered(1)),
            pl.BlockSpec(spatial_w.shape, lambda i: (0, 0, 0), pipeline_mode=pl.Buffered(1)),
            pl.BlockSpec(sb_full.shape, const2, pipeline_mode=pl.Buffered(1)),
            pl.BlockSpec(gv_row.shape, const2),
            pl.BlockSpec(gq_tab.shape, const2),
            pl.BlockSpec(gk_tab.shape, const2),
            pl.BlockSpec(bias_tab.shape, lambda i: (0, 0, 0, 0), pipeline_mode=pl.Buffered(1)),
            *cast_specs,
        ],
        out_specs=[pl.BlockSpec((tm, d), lambda i: (i, 0)), *cast_specs],
        scratch_shapes=[pltpu.VMEM((tm, d), BF16),
                        pltpu.VMEM((B_KV_HEADS, HEAD_DIM, BLOCK), BF16),
                        pltpu.VMEM((B_KV_HEADS, HEAD_DIM, BLOCK), BF16)],
        out_shape=[jax.ShapeDtypeStruct((n, d), F32)]
        + [jax.ShapeDtypeStruct(w.shape, BF16) for w in to_cast],
        compiler_params=pltpu.CompilerParams(
            dimension_semantics=("arbitrary",), vmem_limit_bytes=BIG_VMEM_LIMIT_BYTES),
        name="token_mix",
    )(sinks, x2, mod3, g_mix.reshape(1, d), w_uv, w_qkvt, w_out, spatial_w, sb_full, gv_row,
      gq_tab, gk_tab, bias_tab, *to_cast)


def _tile_sizes(seq, dff):
    tm_ffn = next(t for t in (1024, 512, 256, 128) if seq % t == 0)
    tf = next(t for t in (512, 256, 128) if dff % t == 0)
    tm_mix = next(t for t in (512, 256, 128) if seq % t == 0)
    return tm_ffn, tf, tm_mix


def kernel(x, c, w_ada, b_ada, g_ffn1, w1_ffn1, w3_ffn1, w2_ffn1, g_mix, w_in, spatial_w, spatial_b,
           g_v, g_q, g_k, sinks, rel_bias, w_out, g_ffn2, w1_ffn2, w3_ffn2, w2_ffn2):
    bsz, seq, d = x.shape
    depth = w_ada.shape[0]
    dff = w1_ffn1.shape[-1]
    assert seq % BLOCK == 0 and w_in.shape[-1] == 2 * A_WIDTH + QKV_WIDTH
    tm_ffn, tf, tm_mix = _tile_sizes(seq, dff)
    ada_tn = next(t for t in (1024, 512, 256, 128) if (N_MOD * d) % t == 0)

    bias_tab = _bias_table(rel_bias)
    x2 = x.reshape(bsz * seq, d)
    for l in range(depth):
        mod3 = _ada_mod(c, w_ada[l], b_ada[l], tn=ada_tn).reshape(bsz, N_MOD, d)
        head_tile, w1a, w3a, w2a = _ffn_head(x2, mod3, g_ffn1[l], w1_ffn1[l], w3_ffn1[l], w2_ffn1[l],
                                             mod_row=0, tm=tm_ffn)
        x2 = _ffn(x2, mod3, g_ffn1[l], w1a, w3a, w2a, mod_row=0, seq=seq, tm=tm_ffn, tf=tf,
                  head_tile=head_tile)
        w_uv, w_qkvt = _w_in_prep(w_in[l])
        x2, w1b, w3b, w2b = _mix(
            x2, mod3, g_mix[l], w_uv, w_qkvt, w_out[l].astype(BF16),
            spatial_w[l],
            jnp.repeat(spatial_b[l].T, A_HEAD_DIM, axis=1),
            g_v[l].reshape(1, A_WIDTH),
            jnp.broadcast_to(g_q[l][:, None], (HEAD_DIM, BLOCK)),
            jnp.broadcast_to(g_k[l][:, None], (HEAD_DIM, BLOCK)),
            sinks[l], bias_tab,
            (w1_ffn2[l], w3_ffn2[l], w2_ffn2[l]), seq=seq, tm=tm_mix)
        x2 = _ffn(x2, mod3, g_ffn2[l], w1b, w3b, w2b,
                  mod_row=6, seq=seq, tm=tm_ffn, tf=tf)
    return x2.reshape(bsz, seq, d)
```

```python
import functools
import math

import jax
import jax.numpy as jnp
import numpy as np
from jax import lax
from jax.experimental import pallas as pl
from jax.experimental.pallas import tpu as pltpu

BLOCK = 128
A_HEADS = 8
A_HEAD_DIM = 128
A_WIDTH = A_HEADS * A_HEAD_DIM
B_HEADS = 16
B_KV_HEADS = 2
GROUP = B_HEADS // B_KV_HEADS
HEAD_DIM = 64
B_WIDTH = B_HEADS * HEAD_DIM
KV_WIDTH = B_KV_HEADS * HEAD_DIM
QKV_WIDTH = B_WIDTH + 2 * KV_WIDTH
N_BUCKETS = 32
MAX_DISTANCE = 128
N_MOD = 9
EPS = 1e-6
MASKED = float("-inf")

V7X_VMEM_LIMIT_BYTES = 60000 * 1024
V7X_VMEM_CAPACITY_BYTES = 64 * 1024 * 1024
BIG_VMEM_LIMIT_BYTES = V7X_VMEM_CAPACITY_BYTES - 1024 * 1024

F32 = jnp.float32
BF16 = jnp.bfloat16


def _dot(a, b):
    return jnp.dot(a, b, preferred_element_type=F32)


def _dot_nt(a, b):
    return lax.dot_general(a, b, (((1,), (1,)), ((), ())), preferred_element_type=F32)


def _dot_tn(a, b):
    return lax.dot_general(a, b, (((0,), (0,)), ((), ())), preferred_element_type=F32)


NORM_ROWS = 16


def _norm_modulate_into(h_ref, x_ref, g, shift, scale):
    gain = g * (1.0 + scale)
    for r0 in range(0, x_ref.shape[0], NORM_ROWS):
        rows = slice(r0, r0 + NORM_ROWS)
        x = x_ref[rows, :]
        ms = jnp.mean(x * x, axis=-1, keepdims=True)
        h_ref[rows, :] = (x * lax.rsqrt(ms + EPS) * gain + shift).astype(BF16)


def _ada_kernel(c_ref, w_ref, b_ref, o_ref):
    c = c_ref[...]
    c_act = (c * jax.nn.sigmoid(c)).astype(BF16)
    o_ref[...] = _dot(c_act, w_ref[...].astype(BF16)) + b_ref[...]


def _ada_mod(c, w_ada, b_ada, *, tn):
    bsz, d = c.shape
    n = w_ada.shape[1]
    return pl.pallas_call(
        _ada_kernel,
        grid=(n // tn,),
        in_specs=[
            pl.BlockSpec((bsz, d), lambda j: (0, 0)),
            pl.BlockSpec((d, tn), lambda j: (0, j)),
            pl.BlockSpec((1, tn), lambda j: (0, j)),
        ],
        out_specs=pl.BlockSpec((bsz, tn), lambda j: (0, j)),
        out_shape=jax.ShapeDtypeStruct((bsz, n), F32),
        compiler_params=pltpu.CompilerParams(
            dimension_semantics=("arbitrary",), vmem_limit_bytes=V7X_VMEM_LIMIT_BYTES),
        name="ada_mod",
    )(c, w_ada, b_ada.reshape(1, n))


def _ffn_kernel(x_hbm, mod_ref, g_ref, w1a_ref, w3a_ref, w2a_ref, w1b_ref, w3b_ref, w2b_ref, head_hbm,
                o_ref, x_buf, h_ref, x_sem, head_sem, *, mod_row, has_head):
    i = pl.program_id(0)
    f = pl.program_id(1)
    tm = x_buf.shape[0]
    first = 1 if has_head else 0
    computes = i >= first

    def x_copy(tile):
        return pltpu.make_async_copy(x_hbm.at[pl.ds(tile * tm, tm), :], x_buf, x_sem)

    def chunk(w1_ref, w3_ref, w2_ref):
        h = h_ref[...]
        a = _dot(h, w1_ref[...])
        b = _dot(h, w3_ref[...])
        p = (a * jax.nn.sigmoid(a) * b).astype(BF16)
        return (0.5 * mod_ref[0, mod_row + 2:mod_row + 3, :]) * _dot(p, w2_ref[...])

    if has_head:
        @pl.when(jnp.logical_and(i == 0, f == 0))
        def _():
            head_copy = pltpu.make_async_copy(head_hbm, o_ref, head_sem)
            head_copy.start()
            head_copy.wait()

    @pl.when(jnp.logical_and(f == 0, computes))
    def _():
        if not has_head:
            @pl.when(i == 0)
            def _():
                x_copy(0).start()

        x_copy(i).wait()
        _norm_modulate_into(h_ref, x_buf, g_ref[...], mod_ref[0, mod_row:mod_row + 1, :],
                            mod_ref[0, mod_row + 1:mod_row + 2, :])
        o_ref[...] = x_buf[...] + chunk(w1a_ref, w3a_ref, w2a_ref)

    @pl.when(jnp.logical_and(f == 1, i + 1 < pl.num_programs(0)))
    def _():
        x_copy(i + 1).start()

    @pl.when(jnp.logical_and(f != 0, computes))
    def _():
        h = h_ref[...]
        half_gate = 0.5 * mod_ref[0, mod_row + 2:mod_row + 3, :]
        a1 = _dot(h, w1a_ref[...])
        b1 = _dot(h, w3a_ref[...])
        a2 = _dot(h, w1b_ref[...])
        p1 = (a1 * jax.nn.sigmoid(a1) * b1).astype(BF16)
        b2 = _dot(h, w3b_ref[...])
        o_ref[...] += half_gate * _dot(p1, w2a_ref[...])
        p2 = (a2 * jax.nn.sigmoid(a2) * b2).astype(BF16)
        o_ref[...] += half_gate * _dot(p2, w2b_ref[...])


FFN_HEAD_TF = 256


def _ffn_head_kernel(x_ref, mod_ref, g_ref, w1_ref, w3_ref, w2_ref, o_ref, w1b_ref, w3b_ref, w2b_ref, h_ref,
                     *, mod_row):
    f = pl.program_id(0)

    def chunk():
        w1b_ref[...] = w1_ref[...].astype(BF16)
        w3b_ref[...] = w3_ref[...].astype(BF16)
        w2b_ref[...] = w2_ref[...].astype(BF16)
        h = h_ref[...]
        a = _dot(h, w1b_ref[...])
        b = _dot(h, w3b_ref[...])
        p = (a * jax.nn.sigmoid(a) * b).astype(BF16)
        return (0.5 * mod_ref[0, mod_row + 2:mod_row + 3, :]) * _dot(p, w2b_ref[...])

    @pl.when(f == 0)
    def _():
        _norm_modulate_into(h_ref, x_ref, g_ref[...], mod_ref[0, mod_row:mod_row + 1, :],
                            mod_ref[0, mod_row + 1:mod_row + 2, :])
        o_ref[...] = x_ref[...] + chunk()

    @pl.when(f != 0)
    def _():
        o_ref[...] += chunk()


def _ffn_head(x2, mod3, g, w1, w3, w2, *, mod_row, tm):
    n, d = x2.shape
    dff = w1.shape[1]
    tf = FFN_HEAD_TF
    up = pl.BlockSpec((d, tf), lambda f: (0, f))
    down = pl.BlockSpec((tf, d), lambda f: (f, 0))
    first_tile = pl.BlockSpec((tm, d), lambda f: (0, 0))
    return pl.pallas_call(
        functools.partial(_ffn_head_kernel, mod_row=mod_row),
        grid=(dff // tf,),
        in_specs=[
            pl.BlockSpec((tm, d), lambda f: (0, 0), pipeline_mode=pl.Buffered(1)),
            pl.BlockSpec((1, N_MOD, d), lambda f: (0, 0, 0)),
            pl.BlockSpec((1, d), lambda f: (0, 0)),
            up, up, down,
        ],
        out_specs=[first_tile, up, up, down],
        out_shape=[jax.ShapeDtypeStruct((tm, d), F32), jax.ShapeDtypeStruct(w1.shape, BF16),
                   jax.ShapeDtypeStruct(w3.shape, BF16), jax.ShapeDtypeStruct(w2.shape, BF16)],
        scratch_shapes=[pltpu.VMEM((tm, d), BF16)],
        compiler_params=pltpu.CompilerParams(
            dimension_semantics=("arbitrary",), vmem_limit_bytes=BIG_VMEM_LIMIT_BYTES),
        name=f"ffn_head_mod{mod_row}",
    )(x2, mod3, g.reshape(1, d), w1, w3, w2)


def _ffn(x2, mod3, g, w1, w3, w2, *, mod_row, seq, tm, tf, head_tile=None):
    n, d = x2.shape
    nf = w1.shape[1] // tf
    assert nf % 2 == 1 and nf >= 3, "one chunk at step 0, then pairs"
    has_head = head_tile is not None
    tiles_per_seq = seq // tm
    first_of_pair = lambda i, f: jnp.where(jnp.logical_and(has_head, i == 0), 0, jnp.maximum(2 * f - 1, 0))
    second_of_pair = lambda i, f: jnp.where(jnp.logical_and(has_head, i == 0), 2, jnp.maximum(2 * f, 2))
    up_a = pl.BlockSpec((d, tf), lambda i, f: (0, first_of_pair(i, f)))
    up_b = pl.BlockSpec((d, tf), lambda i, f: (0, second_of_pair(i, f)))
    down_a = pl.BlockSpec((tf, d), lambda i, f: (first_of_pair(i, f), 0))
    down_b = pl.BlockSpec((tf, d), lambda i, f: (second_of_pair(i, f), 0))
    return pl.pallas_call(
        functools.partial(_ffn_kernel, mod_row=mod_row, has_head=has_head),
        grid=(n // tm, (nf + 1) // 2),
        in_specs=[
            pl.BlockSpec(memory_space=pl.ANY),
            pl.BlockSpec((1, N_MOD, d), lambda i, f: (i // tiles_per_seq, 0, 0)),
            pl.BlockSpec((1, d), lambda i, f: (0, 0)),
            up_a, up_a, down_a, up_b, up_b, down_b,
            pl.BlockSpec(memory_space=pl.ANY),
        ],
        out_specs=pl.BlockSpec((tm, d), lambda i, f: (i, 0)),
        out_shape=jax.ShapeDtypeStruct((n, d), F32),
        scratch_shapes=[pltpu.VMEM((tm, d), F32), pltpu.VMEM((tm, d), BF16),
                        pltpu.SemaphoreType.DMA(()), pltpu.SemaphoreType.DMA(())],
        compiler_params=pltpu.CompilerParams(
            dimension_semantics=("arbitrary", "arbitrary"),
            vmem_limit_bytes=BIG_VMEM_LIMIT_BYTES),
        name=f"ffn_mod{mod_row}",
    )(x2, mod3, g.reshape(1, d), w1, w3, w2, w1, w3, w2, head_tile if has_head else x2)


def _t5_bucket_table():
    kj = np.arange(2 * BLOCK)[:, None]
    qi = np.arange(BLOCK)[None, :]
    dist = qi + BLOCK - kj
    in_window = (dist >= 0) & (dist < BLOCK)
    n = np.clip(dist, 0, None)
    max_exact = N_BUCKETS // 2
    nf = np.maximum(n, 1).astype(np.float32)
    large = max_exact + (np.log(nf / np.float32(max_exact)) / np.float32(math.log(MAX_DISTANCE / max_exact))
                         * np.float32(N_BUCKETS - max_exact)).astype(np.int32)
    large = np.minimum(large, N_BUCKETS - 1)
    bucket = np.where(n < max_exact, n, large)
    return np.where(in_window, bucket, -1).astype(np.int32)


def _bias_kernel(rb_ref, bucket_ref, o_ref):
    bucket = bucket_ref[...]
    prev_rows = lax.broadcasted_iota(jnp.int32, bucket.shape, 0) < BLOCK
    for head in range(B_HEADS):
        acc = jnp.full(bucket.shape, MASKED, F32)
        for b in range(N_BUCKETS):
            acc = jnp.where(bucket == b, rb_ref[b, head], acc)
        j, hl = divmod(head, GROUP)
        o_ref[0, j, :, hl * BLOCK:(hl + 1) * BLOCK] = acc
        o_ref[1, j, :, hl * BLOCK:(hl + 1) * BLOCK] = jnp.where(prev_rows, MASKED, acc)


def _bias_table(rel_bias):
    return pl.pallas_call(
        _bias_kernel,
        in_specs=[
            pl.BlockSpec(memory_space=pltpu.SMEM),
            pl.BlockSpec((2 * BLOCK, BLOCK), lambda: (0, 0)),
        ],
        out_specs=pl.BlockSpec((2, B_KV_HEADS, 2 * BLOCK, GROUP * BLOCK), lambda: (0, 0, 0, 0)),
        out_shape=jax.ShapeDtypeStruct((2, B_KV_HEADS, 2 * BLOCK, GROUP * BLOCK), F32),
        name="rel_bias_table",
    )(rel_bias, jnp.asarray(_t5_bucket_table()))


W_IN_PREP_ROWS = 256


def _w_in_prep_kernel(w_ref, uv_ref, qkvt_ref):
    w = w_ref[...]
    uv_ref[...] = w[:, :2 * A_WIDTH].astype(BF16)
    qkvt_ref[...] = w[:, 2 * A_WIDTH:].T.astype(BF16)


def _w_in_prep(w_in):
    d, cols = w_in.shape
    rows = min(W_IN_PREP_ROWS, d)
    return pl.pallas_call(
        _w_in_prep_kernel,
        grid=(d // rows,),
        in_specs=[pl.BlockSpec((rows, cols), lambda r: (r, 0))],
        out_specs=[pl.BlockSpec((rows, 2 * A_WIDTH), lambda r: (r, 0)),
                   pl.BlockSpec((QKV_WIDTH, rows), lambda r: (0, r))],
        out_shape=[jax.ShapeDtypeStruct((d, 2 * A_WIDTH), BF16),
                   jax.ShapeDtypeStruct((QKV_WIDTH, d), BF16)],
        compiler_params=pltpu.CompilerParams(dimension_semantics=("arbitrary",)),
        name="w_in_prep",
    )(w_in)


def _gelu(x):
    return 0.5 * x * (1.0 + lax.erf(x * (1.0 / math.sqrt(2.0))))


def _rms_rows(xt, gain):
    ms = jnp.mean(xt * xt, axis=0, keepdims=True)
    return xt * lax.rsqrt(ms + EPS) * gain


def _mix_kernel(sinks_ref, x_ref, mod_ref, g_ref, wuv_ref, wqkvt_ref, wout_ref, sw_ref, sb_ref,
                gv_ref, gq_ref, gk_ref, bias_ref, cast_a_ref, cast_b_ref, cast_c_ref,
                o_ref, cast_a_out, cast_b_out, cast_c_out, h_ref, kprev_ref, vprev_ref, *, tiles_per_seq):
    tm = x_ref.shape[0]
    nblk = tm // BLOCK
    first = pl.program_id(0) % tiles_per_seq == 0

    @pl.when(first)
    def _():
        kprev_ref[...] = jnp.zeros_like(kprev_ref)
        vprev_ref[...] = jnp.zeros_like(vprev_ref)

    _norm_modulate_into(h_ref, x_ref, g_ref[...], mod_ref[0, 3:4, :], mod_ref[0, 4:5, :])
    h = h_ref[...]

    zuv = _dot(h, wuv_ref[...])

    for src, dst in ((cast_a_ref, cast_a_out), (cast_b_ref, cast_b_out), (cast_c_ref, cast_c_out)):
        dst[...] = src[...].astype(BF16)

    u = _gelu(zuv[:, :A_WIDTH])
    v = _gelu(zuv[:, A_WIDTH:])
    row = lax.broadcasted_iota(jnp.int32, (BLOCK, BLOCK), 0)
    col = lax.broadcasted_iota(jnp.int32, (BLOCK, BLOCK), 1)
    causal = row >= col
    ya_heads = []
    for hd in range(A_HEADS):
        cols = slice(hd * A_HEAD_DIM, (hd + 1) * A_HEAD_DIM)
        vh = v[:, cols]
        ms = jnp.mean(vh * vh, axis=-1, keepdims=True)
        vn = (vh * lax.rsqrt(ms + EPS) * gv_ref[:, cols]).astype(BF16)
        w_tril = jnp.where(causal, sw_ref[hd], 0.0).astype(BF16)
        mixed = [_dot(w_tril, vn[c * BLOCK:(c + 1) * BLOCK]) + sb_ref[:, cols] for c in range(nblk)]
        ya_heads.append(u[:, cols] * jnp.concatenate(mixed, axis=0))
    ya = jnp.concatenate(ya_heads, axis=1).astype(BF16)

    gate = mod_ref[0, 5:6, :]
    d_model = o_ref.shape[1]
    slab = max(d_model // (nblk * B_KV_HEADS), BLOCK)
    n_slabs = d_model // slab

    def out_proj_a(unit):
        if unit < n_slabs:
            cols = slice(unit * slab, (unit + 1) * slab)
            o_ref[:, cols] = x_ref[:, cols] + gate[:, cols] * _dot(ya, wout_ref[:A_WIDTH, cols])

    zt = _dot_nt(wqkvt_ref[...], h)
    out_proj_a(0)
    gq = gq_ref[...] * (HEAD_DIM ** -0.5)
    qn = [_rms_rows(zt[hd * HEAD_DIM:(hd + 1) * HEAD_DIM], jnp.tile(gq, (1, nblk))).astype(BF16)
          for hd in range(B_HEADS)]
    kn, vv = [], []
    for j in range(B_KV_HEADS):
        k_rows = slice(B_WIDTH + j * HEAD_DIM, B_WIDTH + (j + 1) * HEAD_DIM)
        v_rows = slice(B_WIDTH + KV_WIDTH + j * HEAD_DIM, B_WIDTH + KV_WIDTH + (j + 1) * HEAD_DIM)
        kn.append(_rms_rows(zt[k_rows], jnp.tile(gk_ref[...], (1, nblk))).astype(BF16))
        vv.append(zt[v_rows].astype(BF16))

    variant = jnp.where(first, 1, 0)
    yb_blocks = []
    for c in range(nblk):
        tok = slice(c * BLOCK, (c + 1) * BLOCK)
        head_rows = [None] * B_HEADS
        for j in range(B_KV_HEADS):
            if c == 0:
                k_prev, v_prev = kprev_ref[j], vprev_ref[j]
                bias = bias_ref[variant, j]
            else:
                prev = slice((c - 1) * BLOCK, c * BLOCK)
                k_prev, v_prev = kn[j][:, prev], vv[j][:, prev]
                bias = bias_ref[0, j]
            k_band = jnp.concatenate([k_prev, kn[j][:, tok]], axis=1)
            v_band = jnp.concatenate([v_prev, vv[j][:, tok]], axis=1)
            q_grp = jnp.concatenate([qn[j * GROUP + hl][:, tok] for hl in range(GROUP)], axis=1)
            s = _dot_tn(k_band, q_grp) + bias
            out_proj_a(c * B_KV_HEADS + j + 1)
            sink = jnp.concatenate(
                [jnp.full((1, BLOCK), sinks_ref[j * GROUP + hl], F32) for hl in range(GROUP)], axis=1)
            m = jnp.maximum(jnp.max(s, axis=0, keepdims=True), sink)
            p = jnp.exp(s - m)
            denom = jnp.sum(p, axis=0, keepdims=True) + jnp.exp(sink - m)
            o_t = _dot(v_band, p.astype(BF16)) / denom
            for hl in range(GROUP):
                head_rows[j * GROUP + hl] = o_t[:, hl * BLOCK:(hl + 1) * BLOCK]
        yb_blocks.append(jnp.concatenate(head_rows, axis=0))
    yb = jnp.concatenate(yb_blocks, axis=1).T

    last = slice((nblk - 1) * BLOCK, nblk * BLOCK)
    for j in range(B_KV_HEADS):
        kprev_ref[j] = kn[j][:, last]
        vprev_ref[j] = vv[j][:, last]

    o_ref[...] += gate * _dot(yb.astype(BF16), wout_ref[A_WIDTH:, :])


def _cast_block_spec(shape, steps):
    rows, cols = shape
    for row_parts in range(steps, 0, -1):
        col_parts = steps // row_parts
        if (row_parts * col_parts == steps and rows % (16 * row_parts) == 0
                and cols % (128 * col_parts) == 0):
            return pl.BlockSpec((rows // row_parts, cols // col_parts),
                                lambda i: (i // col_parts, i % col_parts))
    raise ValueError(f"no {steps}-step tiling of {shape}")


def _mix(x2, mod3, g_mix, w_uv, w_qkvt, w_out, spatial_w, sb_full, gv_row, gq_tab, gk_tab, sinks,
         bias_tab, to_cast, *, seq, tm):
    n, d = x2.shape
    tiles_per_seq = seq // tm
    steps = n // tm
    const2 = lambda i: (0, 0)
    cast_specs = [_cast_block_spec(w.shape, steps) for w in to_cast]
    return pl.pallas_call(
        functools.partial(_mix_kernel, tiles_per_seq=tiles_per_seq),
        grid=(n // tm,),
        in_specs=[
            pl.BlockSpec(memory_space=pltpu.SMEM),
            pl.BlockSpec((tm, d), lambda i: (i, 0)),
            pl.BlockSpec((1, N_MOD, d), lambda i: (i // tiles_per_seq, 0, 0)),
            pl.BlockSpec((1, d), const2),
            pl.BlockSpec(w_uv.shape, const2, pipeline_mode=pl.Buffered(1)),
            pl.BlockSpec(w_qkvt.shape, const2, pipeline_mode=pl.Buffered(1)),
            pl.BlockSpec(w_out.shape, const2, pipeline_mode=pl.Buffered(1)),
            pl.BlockSpec(spatial_w.shape, lambda i: (0, 0, 0), pipeline_mode=pl.Buffered(1)),
            pl.BlockSpec(sb_full.shape, const2, pipeline_mode=pl.Buffered(1)),
            pl.BlockSpec(gv_row.shape, const2),
            pl.BlockSpec(gq_tab.shape, const2),
            pl.BlockSpec(gk_tab.shape, const2),
            pl.BlockSpec(bias_tab.shape, lambda i: (0, 0, 0, 0), pipeline_mode=pl.Buffered(1)),
            *cast_specs,
        ],
        out_specs=[pl.BlockSpec((tm, d), lambda i: (i, 0)), *cast_specs],
        scratch_shapes=[pltpu.VMEM((tm, d), BF16),
                        pltpu.VMEM((B_KV_HEADS, HEAD_DIM, BLOCK), BF16),
                        pltpu.VMEM((B_KV_HEADS, HEAD_DIM, BLOCK), BF16)],
        out_shape=[jax.ShapeDtypeStruct((n, d), F32)]
        + [jax.ShapeDtypeStruct(w.shape, BF16) for w in to_cast],
        compiler_params=pltpu.CompilerParams(
            dimension_semantics=("arbitrary",), vmem_limit_bytes=BIG_VMEM_LIMIT_BYTES),
        name="token_mix",
    )(sinks, x2, mod3, g_mix.reshape(1, d), w_uv, w_qkvt, w_out, spatial_w, sb_full, gv_row,
      gq_tab, gk_tab, bias_tab, *to_cast)


def _tile_sizes(seq, dff):
    tm_ffn = next(t for t in (1024, 512, 256, 128) if seq % t == 0)
    tf = next(t for t in (512, 256, 128) if dff % t == 0)
    tm_mix = next(t for t in (512, 256, 128) if seq % t == 0)
    return tm_ffn, tf, tm_mix


def kernel(x, c, w_ada, b_ada, g_ffn1, w1_ffn1, w3_ffn1, w2_ffn1, g_mix, w_in, spatial_w, spatial_b,
           g_v, g_q, g_k, sinks, rel_bias, w_out, g_ffn2, w1_ffn2, w3_ffn2, w2_ffn2):
    bsz, seq, d = x.shape
    depth = w_ada.shape[0]
    dff = w1_ffn1.shape[-1]
    assert seq % BLOCK == 0 and w_in.shape[-1] == 2 * A_WIDTH + QKV_WIDTH
    tm_ffn, tf, tm_mix = _tile_sizes(seq, dff)
    ada_tn = next(t for t in (1024, 512, 256, 128) if (N_MOD * d) % t == 0)

    bias_tab = _bias_table(rel_bias)
    x2 = x.reshape(bsz * seq, d)
    for l in range(depth):
        mod3 = _ada_mod(c, w_ada[l], b_ada[l], tn=ada_tn).reshape(bsz, N_MOD, d)
        head_tile, w1a, w3a, w2a = _ffn_head(x2, mod3, g_ffn1[l], w1_ffn1[l], w3_ffn1[l], w2_ffn1[l],
                                             mod_row=0, tm=tm_ffn)
        x2 = _ffn(x2, mod3, g_ffn1[l], w1a, w3a, w2a, mod_row=0, seq=seq, tm=tm_ffn, tf=tf,
                  head_tile=head_tile)
        w_uv, w_qkvt = _w_in_prep(w_in[l])
        x2, w1b, w3b, w2b = _mix(
            x2, mod3, g_mix[l], w_uv, w_qkvt, w_out[l].astype(BF16),
            spatial_w[l],
            jnp.repeat(spatial_b[l].T, A_HEAD_DIM, axis=1),
            g_v[l].reshape(1, A_WIDTH),
            jnp.broadcast_to(g_q[l][:, None], (HEAD_DIM, BLOCK)),
            jnp.broadcast_to(g_k[l][:, None], (HEAD_DIM, BLOCK)),
            sinks[l], bias_tab,
            (w1_ffn2[l], w3_ffn2[l], w2_ffn2[l]), seq=seq, tm=tm_mix)
        x2 = _ffn(x2, mod3, g_ffn2[l], w1b, w3b, w2b,
                  mod_row=6, seq=seq, tm=tm_ffn, tf=tf)
    return x2.reshape(bsz, seq, d)
```

```python
import functools
import math

import jax
import jax.numpy as jnp
import numpy as np
from jax import lax
from jax.experimental import pallas as pl
from jax.experimental.pallas import tpu as pltpu

BLOCK = 128
A_HEADS = 8
A_HEAD_DIM = 128
A_WIDTH = A_HEADS * A_HEAD_DIM
B_HEADS = 16
B_KV_HEADS = 2
GROUP = B_HEADS // B_KV_HEADS
HEAD_DIM = 64
B_WIDTH = B_HEADS * HEAD_DIM
KV_WIDTH = B_KV_HEADS * HEAD_DIM
QKV_WIDTH = B_WIDTH + 2 * KV_WIDTH
N_BUCKETS = 32
MAX_DISTANCE = 128
N_MOD = 9
EPS = 1e-6
MASKED = float("-inf")

V7X_VMEM_LIMIT_BYTES = 60000 * 1024
V7X_VMEM_CAPACITY_BYTES = 64 * 1024 * 1024
BIG_VMEM_LIMIT_BYTES = V7X_VMEM_CAPACITY_BYTES - 1024 * 1024

F32 = jnp.float32
BF16 = jnp.bfloat16


def _dot(a, b):
    return jnp.dot(a, b, preferred_element_type=F32)


def _dot_nt(a, b):
    return lax.dot_general(a, b, (((1,), (1,)), ((), ())), preferred_element_type=F32)


def _dot_tn(a, b):
    return lax.dot_general(a, b, (((0,), (0,)), ((), ())), preferred_element_type=F32)


NORM_ROWS = 16


def _norm_modulate_into(h_ref, x_ref, g, shift, scale):
    gain = g * (1.0 + scale)
    for r0 in range(0, x_ref.shape[0], NORM_ROWS):
        rows = slice(r0, r0 + NORM_ROWS)
        x = x_ref[rows, :]
        ms = jnp.mean(x * x, axis=-1, keepdims=True)
        h_ref[rows, :] = (x * lax.rsqrt(ms + EPS) * gain + shift).astype(BF16)


def _ada_kernel(c_ref, w_ref, b_ref, o_ref):
    c = c_ref[...]
    c_act = (c * jax.nn.sigmoid(c)).astype(BF16)
    o_ref[...] = _dot(c_act, w_ref[...].astype(BF16)) + b_ref[...]


def _ada_mod(c, w_ada, b_ada, *, tn, n):
    bsz, d = c.shape
    return pl.pallas_call(
        _ada_kernel,
        grid=(n // tn,),
        in_specs=[
            pl.BlockSpec((bsz, d), lambda j: (0, 0)),
            pl.BlockSpec((d, tn), lambda j: (0, j)),
            pl.BlockSpec((1, tn), lambda j: (0, j)),
        ],
        out_specs=pl.BlockSpec((bsz, tn), lambda j: (0, j)),
        out_shape=jax.ShapeDtypeStruct((bsz, n), F32),
        compiler_params=pltpu.CompilerParams(
            dimension_semantics=("arbitrary",), vmem_limit_bytes=V7X_VMEM_LIMIT_BYTES),
        name="ada_mod",
    )(c, w_ada, b_ada.reshape(1, -1))


def _ffn_kernel(x_hbm, mod_ref, g_ref, w1a_ref, w3a_ref, w2a_ref, w1b_ref, w3b_ref, w2b_ref, head_hbm,
                o_ref, x_buf, h_ref, x_sem, head_sem, *, mod_row, has_head):
    i = pl.program_id(0)
    f = pl.program_id(1)
    tm = x_buf.shape[0]
    first = 1 if has_head else 0
    computes = i >= first

    def x_copy(tile):
        return pltpu.make_async_copy(x_hbm.at[pl.ds(tile * tm, tm), :], x_buf, x_sem)

    def chunk(w1_ref, w3_ref, w2_ref):
        h = h_ref[...]
        a = _dot(h, w1_ref[...])
        b = _dot(h, w3_ref[...])
        p = (a * jax.nn.sigmoid(a) * b).astype(BF16)
        return (0.5 * mod_ref[0, mod_row + 2:mod_row + 3, :]) * _dot(p, w2_ref[...])

    if has_head:
        @pl.when(jnp.logical_and(i == 0, f == 0))
        def _():
            head_copy = pltpu.make_async_copy(head_hbm, o_ref, head_sem)
            head_copy.start()
            head_copy.wait()

    @pl.when(jnp.logical_and(f == 0, computes))
    def _():
        if not has_head:
            @pl.when(i == 0)
            def _():
                x_copy(0).start()

        x_copy(i).wait()
        _norm_modulate_into(h_ref, x_buf, g_ref[...], mod_ref[0, mod_row:mod_row + 1, :],
                            mod_ref[0, mod_row + 1:mod_row + 2, :])
        o_ref[...] = x_buf[...] + chunk(w1a_ref, w3a_ref, w2a_ref)

    @pl.when(jnp.logical_and(f == 1, i + 1 < pl.num_programs(0)))
    def _():
        x_copy(i + 1).start()

    @pl.when(jnp.logical_and(f != 0, computes))
    def _():
        h = h_ref[...]
        half_gate = 0.5 * mod_ref[0, mod_row + 2:mod_row + 3, :]
        a1 = _dot(h, w1a_ref[...])
        b1 = _dot(h, w3a_ref[...])
        a2 = _dot(h, w1b_ref[...])
        p1 = (a1 * jax.nn.sigmoid(a1) * b1).astype(BF16)
        b2 = _dot(h, w3b_ref[...])
        o_ref[...] += half_gate * _dot(p1, w2a_ref[...])
        p2 = (a2 * jax.nn.sigmoid(a2) * b2).astype(BF16)
        o_ref[...] += half_gate * _dot(p2, w2b_ref[...])


FFN_HEAD_TF = 256


def _ffn_head_kernel(x_ref, mod_ref, g_ref, w1_ref, w3_ref, w2_ref, o_ref, w1b_ref, w3b_ref, w2b_ref, h_ref,
                     *, mod_row):
    f = pl.program_id(0)

    def chunk():
        w1b_ref[...] = w1_ref[...].astype(BF16)
        w3b_ref[...] = w3_ref[...].astype(BF16)
        w2b_ref[...] = w2_ref[...].astype(BF16)
        h = h_ref[...]
        a = _dot(h, w1b_ref[...])
        b = _dot(h, w3b_ref[...])
        p = (a * jax.nn.sigmoid(a) * b).astype(BF16)
        return (0.5 * mod_ref[0, mod_row + 2:mod_row + 3, :]) * _dot(p, w2b_ref[...])

    @pl.when(f == 0)
    def _():
        _norm_modulate_into(h_ref, x_ref, g_ref[...], mod_ref[0, mod_row:mod_row + 1, :],
                            mod_ref[0, mod_row + 1:mod_row + 2, :])
        o_ref[...] = x_ref[...] + chunk()

    @pl.when(f != 0)
    def _():
        o_ref[...] += chunk()


def _ffn_head(x2, mod3, g, w1, w3, w2, *, mod_row, tm):
    n, d = x2.shape
    dff = w1.shape[1]
    tf = FFN_HEAD_TF
    up = pl.BlockSpec((d, tf), lambda f: (0, f))
    down = pl.BlockSpec((tf, d), lambda f: (f, 0))
    first_tile = pl.BlockSpec((tm, d), lambda f: (0, 0))
    return pl.pallas_call(
        functools.partial(_ffn_head_kernel, mod_row=mod_row),
        grid=(dff // tf,),
        in_specs=[
            pl.BlockSpec((tm, d), lambda f: (0, 0), pipeline_mode=pl.Buffered(1)),
            pl.BlockSpec((1, mod3.shape[1], d), lambda f: (0, 0, 0)),
            pl.BlockSpec((1, d), lambda f: (0, 0)),
            up, up, down,
        ],
        out_specs=[first_tile, up, up, down],
        out_shape=[jax.ShapeDtypeStruct((tm, d), F32), jax.ShapeDtypeStruct(w1.shape, BF16),
                   jax.ShapeDtypeStruct(w3.shape, BF16), jax.ShapeDtypeStruct(w2.shape, BF16)],
        scratch_shapes=[pltpu.VMEM((tm, d), BF16)],
        compiler_params=pltpu.CompilerParams(
            dimension_semantics=("arbitrary",), vmem_limit_bytes=BIG_VMEM_LIMIT_BYTES),
        name=f"ffn_head_mod{mod_row}",
    )(x2, mod3, g.reshape(1, d), w1, w3, w2)


def _ffn(x2, mod3, g, w1, w3, w2, *, mod_row, seq, tm, tf, head_tile=None):
    n, d = x2.shape
    nf = w1.shape[1] // tf
    assert nf % 2 == 1 and nf >= 3, "one chunk at step 0, then pairs"
    has_head = head_tile is not None
    tiles_per_seq = seq // tm
    first_of_pair = lambda i, f: jnp.where(jnp.logical_and(has_head, i == 0), 0, jnp.maximum(2 * f - 1, 0))
    second_of_pair = lambda i, f: jnp.where(jnp.logical_and(has_head, i == 0), 2, jnp.maximum(2 * f, 2))
    up_a = pl.BlockSpec((d, tf), lambda i, f: (0, first_of_pair(i, f)))
    up_b = pl.BlockSpec((d, tf), lambda i, f: (0, second_of_pair(i, f)))
    down_a = pl.BlockSpec((tf, d), lambda i, f: (first_of_pair(i, f), 0))
    down_b = pl.BlockSpec((tf, d), lambda i, f: (second_of_pair(i, f), 0))
    return pl.pallas_call(
        functools.partial(_ffn_kernel, mod_row=mod_row, has_head=has_head),
        grid=(n // tm, (nf + 1) // 2),
        in_specs=[
            pl.BlockSpec(memory_space=pl.ANY),
            pl.BlockSpec((1, mod3.shape[1], d), lambda i, f: (i // tiles_per_seq, 0, 0)),
            pl.BlockSpec((1, d), lambda i, f: (0, 0)),
            up_a, up_a, down_a, up_b, up_b, down_b,
            pl.BlockSpec(memory_space=pl.ANY),
        ],
        out_specs=pl.BlockSpec((tm, d), lambda i, f: (i, 0)),
        out_shape=jax.ShapeDtypeStruct((n, d), F32),
        scratch_shapes=[pltpu.VMEM((tm, d), F32), pltpu.VMEM((tm, d), BF16),
                        pltpu.SemaphoreType.DMA(()), pltpu.SemaphoreType.DMA(())],
        compiler_params=pltpu.CompilerParams(
            dimension_semantics=("arbitrary", "arbitrary"),
            vmem_limit_bytes=BIG_VMEM_LIMIT_BYTES),
        name=f"ffn_mod{mod_row}",
    )(x2, mod3, g.reshape(1, d), w1, w3, w2, w1, w3, w2, head_tile if has_head else x2)


def _t5_bucket_table():
    kj = np.arange(2 * BLOCK)[:, None]
    qi = np.arange(BLOCK)[None, :]
    dist = qi + BLOCK - kj
    in_window = (dist >= 0) & (dist < BLOCK)
    n = np.clip(dist, 0, None)
    max_exact = N_BUCKETS // 2
    nf = np.maximum(n, 1).astype(np.float32)
    large = max_exact + (np.log(nf / np.float32(max_exact)) / np.float32(math.log(MAX_DISTANCE / max_exact))
                         * np.float32(N_BUCKETS - max_exact)).astype(np.int32)
    large = np.minimum(large, N_BUCKETS - 1)
    bucket = np.where(n < max_exact, n, large)
    return np.where(in_window, bucket, -1).astype(np.int32)


def _bias_kernel(rb_ref, bucket_ref, o_ref):
    bucket = bucket_ref[...]
    prev_rows = lax.broadcasted_iota(jnp.int32, bucket.shape, 0) < BLOCK
    for head in range(B_HEADS):
        acc = jnp.full(bucket.shape, MASKED, F32)
        for b in range(N_BUCKETS):
            acc = jnp.where(bucket == b, rb_ref[b, head], acc)
        j, hl = divmod(head, GROUP)
        o_ref[0, j, :, hl * BLOCK:(hl + 1) * BLOCK] = acc
        o_ref[1, j, :, hl * BLOCK:(hl + 1) * BLOCK] = jnp.where(prev_rows, MASKED, acc)


def _bias_table(rel_bias):
    return pl.pallas_call(
        _bias_kernel,
        in_specs=[
            pl.BlockSpec(memory_space=pltpu.SMEM),
            pl.BlockSpec((2 * BLOCK, BLOCK), lambda: (0, 0)),
        ],
        out_specs=pl.BlockSpec((2, B_KV_HEADS, 2 * BLOCK, GROUP * BLOCK), lambda: (0, 0, 0, 0)),
        out_shape=jax.ShapeDtypeStruct((2, B_KV_HEADS, 2 * BLOCK, GROUP * BLOCK), F32),
        name="rel_bias_table",
    )(rel_bias, jnp.asarray(_t5_bucket_table()))


W_IN_PREP_ROWS = 256


def _w_in_prep_kernel(w_ref, uv_ref, qkvt_ref):
    w = w_ref[...]
    uv_ref[...] = w[:, :2 * A_WIDTH].astype(BF16)
    qkvt_ref[...] = w[:, 2 * A_WIDTH:].T.astype(BF16)


def _w_in_prep(w_in):
    d, cols = w_in.shape
    rows = min(W_IN_PREP_ROWS, d)
    return pl.pallas_call(
        _w_in_prep_kernel,
        grid=(d // rows,),
        in_specs=[pl.BlockSpec((rows, cols), lambda r: (r, 0))],
        out_specs=[pl.BlockSpec((rows, 2 * A_WIDTH), lambda r: (r, 0)),
                   pl.BlockSpec((QKV_WIDTH, rows), lambda r: (0, r))],
        out_shape=[jax.ShapeDtypeStruct((d, 2 * A_WIDTH), BF16),
                   jax.ShapeDtypeStruct((QKV_WIDTH, d), BF16)],
        compiler_params=pltpu.CompilerParams(dimension_semantics=("arbitrary",)),
        name="w_in_prep",
    )(w_in)


def _gelu(x):
    return 0.5 * x * (1.0 + lax.erf(x * (1.0 / math.sqrt(2.0))))


def _rms_rows(xt, gain):
    ms = jnp.mean(xt * xt, axis=0, keepdims=True)
    return xt * lax.rsqrt(ms + EPS) * gain


def _mix_kernel(sinks_ref, x_ref, mod_ref, g_ref, wuv_ref, wqkvt_ref, wout_ref, sw_ref, sb_ref,
                gv_ref, gq_ref, gk_ref, bias_ref, cast_a_ref, cast_b_ref, cast_c_ref,
                c_ref, wada_ref, bada_ref,
                o_ref, cast_a_out, cast_b_out, cast_c_out, mod_tail_out,
                h_ref, kprev_ref, vprev_ref, *, tiles_per_seq):
    tm = x_ref.shape[0]
    nblk = tm // BLOCK
    first = pl.program_id(0) % tiles_per_seq == 0

    @pl.when(first)
    def _():
        kprev_ref[...] = jnp.zeros_like(kprev_ref)
        vprev_ref[...] = jnp.zeros_like(vprev_ref)

    _norm_modulate_into(h_ref, x_ref, g_ref[...], mod_ref[0, 3:4, :], mod_ref[0, 4:5, :])
    h = h_ref[...]

    zuv = _dot(h, wuv_ref[...])

    for src, dst in ((cast_a_ref, cast_a_out), (cast_b_ref, cast_b_out), (cast_c_ref, cast_c_out)):
        dst[...] = src[...].astype(BF16)
    c = c_ref[...]
    c_act = (c * jax.nn.sigmoid(c)).astype(BF16)
    mod_tail_out[...] = _dot(c_act, wada_ref[...].astype(BF16)) + bada_ref[...]

    u = _gelu(zuv[:, :A_WIDTH])
    v = _gelu(zuv[:, A_WIDTH:])
    row = lax.broadcasted_iota(jnp.int32, (BLOCK, BLOCK), 0)
    col = lax.broadcasted_iota(jnp.int32, (BLOCK, BLOCK), 1)
    causal = row >= col
    ya_heads = []
    for hd in range(A_HEADS):
        cols = slice(hd * A_HEAD_DIM, (hd + 1) * A_HEAD_DIM)
        vh = v[:, cols]
        ms = jnp.mean(vh * vh, axis=-1, keepdims=True)
        vn = (vh * lax.rsqrt(ms + EPS) * gv_ref[:, cols]).astype(BF16)
        w_tril = jnp.where(causal, sw_ref[hd], 0.0).astype(BF16)
        mixed = [_dot(w_tril, vn[c * BLOCK:(c + 1) * BLOCK]) + sb_ref[:, cols] for c in range(nblk)]
        ya_heads.append(u[:, cols] * jnp.concatenate(mixed, axis=0))
    ya = jnp.concatenate(ya_heads, axis=1).astype(BF16)

    gate = mod_ref[0, 5:6, :]
    d_model = o_ref.shape[1]
    slab = max(d_model // (nblk * B_KV_HEADS), BLOCK)
    n_slabs = d_model // slab

    def out_proj_a(unit):
        if unit < n_slabs:
            cols = slice(unit * slab, (unit + 1) * slab)
            o_ref[:, cols] = x_ref[:, cols] + gate[:, cols] * _dot(ya, wout_ref[:A_WIDTH, cols])

    zt = _dot_nt(wqkvt_ref[...], h)
    out_proj_a(0)
    gq = gq_ref[...] * (HEAD_DIM ** -0.5)
    qn = [_rms_rows(zt[hd * HEAD_DIM:(hd + 1) * HEAD_DIM], jnp.tile(gq, (1, nblk))).astype(BF16)
          for hd in range(B_HEADS)]
    kn, vv = [], []
    for j in range(B_KV_HEADS):
        k_rows = slice(B_WIDTH + j * HEAD_DIM, B_WIDTH + (j + 1) * HEAD_DIM)
        v_rows = slice(B_WIDTH + KV_WIDTH + j * HEAD_DIM, B_WIDTH + KV_WIDTH + (j + 1) * HEAD_DIM)
        kn.append(_rms_rows(zt[k_rows], jnp.tile(gk_ref[...], (1, nblk))).astype(BF16))
        vv.append(zt[v_rows].astype(BF16))

    variant = jnp.where(first, 1, 0)
    yb_blocks = []
    for c in range(nblk):
        tok = slice(c * BLOCK, (c + 1) * BLOCK)
        head_rows = [None] * B_HEADS
        for j in range(B_KV_HEADS):
            if c == 0:
                k_prev, v_prev = kprev_ref[j], vprev_ref[j]
                bias = bias_ref[variant, j]
            else:
                prev = slice((c - 1) * BLOCK, c * BLOCK)
                k_prev, v_prev = kn[j][:, prev], vv[j][:, prev]
                bias = bias_ref[0, j]
            k_band = jnp.concatenate([k_prev, kn[j][:, tok]], axis=1)
            v_band = jnp.concatenate([v_prev, vv[j][:, tok]], axis=1)
            q_grp = jnp.concatenate([qn[j * GROUP + hl][:, tok] for hl in range(GROUP)], axis=1)
            s = _dot_tn(k_band, q_grp) + bias
            out_proj_a(c * B_KV_HEADS + j + 1)
            sink = jnp.concatenate(
                [jnp.full((1, BLOCK), sinks_ref[j * GROUP + hl], F32) for hl in range(GROUP)], axis=1)
            m = jnp.maximum(jnp.max(s, axis=0, keepdims=True), sink)
            p = jnp.exp(s - m)
            denom = jnp.sum(p, axis=0, keepdims=True) + jnp.exp(sink - m)
            o_t = _dot(v_band, p.astype(BF16)) / denom
            for hl in range(GROUP):
                head_rows[j * GROUP + hl] = o_t[:, hl * BLOCK:(hl + 1) * BLOCK]
        yb_blocks.append(jnp.concatenate(head_rows, axis=0))
    yb = jnp.concatenate(yb_blocks, axis=1).T

    last = slice((nblk - 1) * BLOCK, nblk * BLOCK)
    for j in range(B_KV_HEADS):
        kprev_ref[j] = kn[j][:, last]
        vprev_ref[j] = vv[j][:, last]

    o_ref[...] += gate * _dot(yb.astype(BF16), wout_ref[A_WIDTH:, :])


def _cast_block_spec(shape, steps):
    rows, cols = shape
    for row_parts in range(steps, 0, -1):
        col_parts = steps // row_parts
        if (row_parts * col_parts == steps and rows % (16 * row_parts) == 0
                and cols % (128 * col_parts) == 0):
            return pl.BlockSpec((rows // row_parts, cols // col_parts),
                                lambda i: (i // col_parts, i % col_parts))
    raise ValueError(f"no {steps}-step tiling of {shape}")


def _mix(x2, mod3, g_mix, w_uv, w_qkvt, w_out, spatial_w, sb_full, gv_row, gq_tab, gk_tab, sinks,
         bias_tab, to_cast, c, w_ada, b_ada, mod_done, *, seq, tm):
    n, d = x2.shape
    tiles_per_seq = seq // tm
    steps = n // tm
    const2 = lambda i: (0, 0)
    cast_specs = [_cast_block_spec(w.shape, steps) for w in to_cast]
    tail = w_ada.shape[1] - mod_done
    tail_blocks = tail // BLOCK
    assert tail % BLOCK == 0 and mod_done % BLOCK == 0 and tail_blocks <= steps
    tail_block = lambda i: jnp.minimum(i, tail_blocks - 1)
    ada_specs = [pl.BlockSpec(c.shape, const2),
                 pl.BlockSpec((d, BLOCK), lambda i: (0, mod_done // BLOCK + tail_block(i))),
                 pl.BlockSpec((1, BLOCK), lambda i: (0, mod_done // BLOCK + tail_block(i)))]
    mod_tail_spec = pl.BlockSpec((c.shape[0], BLOCK), lambda i: (0, tail_block(i)))
    return pl.pallas_call(
        functools.partial(_mix_kernel, tiles_per_seq=tiles_per_seq),
        grid=(n // tm,),
        in_specs=[
            pl.BlockSpec(memory_space=pltpu.SMEM),
            pl.BlockSpec((tm, d), lambda i: (i, 0)),
            pl.BlockSpec((1, mod3.shape[1], d), lambda i: (i // tiles_per_seq, 0, 0)),
            pl.BlockSpec((1, d), const2),
            pl.BlockSpec(w_uv.shape, const2, pipeline_mode=pl.Buffered(1)),
            pl.BlockSpec(w_qkvt.shape, const2, pipeline_mode=pl.Buffered(1)),
            pl.BlockSpec(w_out.shape, const2, pipeline_mode=pl.Buffered(1)),
            pl.BlockSpec(spatial_w.shape, lambda i: (0, 0, 0), pipeline_mode=pl.Buffered(1)),
            pl.BlockSpec(sb_full.shape, const2, pipeline_mode=pl.Buffered(1)),
            pl.BlockSpec(gv_row.shape, const2),
            pl.BlockSpec(gq_tab.shape, const2),
            pl.BlockSpec(gk_tab.shape, const2),
            pl.BlockSpec(bias_tab.shape, lambda i: (0, 0, 0, 0), pipeline_mode=pl.Buffered(1)),
            *cast_specs,
            *ada_specs,
        ],
        out_specs=[pl.BlockSpec((tm, d), lambda i: (i, 0)), *cast_specs, mod_tail_spec],
        scratch_shapes=[pltpu.VMEM((tm, d), BF16),
                        pltpu.VMEM((B_KV_HEADS, HEAD_DIM, BLOCK), BF16),
                        pltpu.VMEM((B_KV_HEADS, HEAD_DIM, BLOCK), BF16)],
        out_shape=[jax.ShapeDtypeStruct((n, d), F32)]
        + [jax.ShapeDtypeStruct(w.shape, BF16) for w in to_cast]
        + [jax.ShapeDtypeStruct((c.shape[0], tail), F32)],
        compiler_params=pltpu.CompilerParams(
            dimension_semantics=("arbitrary",), vmem_limit_bytes=BIG_VMEM_LIMIT_BYTES),
        name="token_mix",
    )(sinks, x2, mod3, g_mix.reshape(1, d), w_uv, w_qkvt, w_out, spatial_w, sb_full, gv_row,
      gq_tab, gk_tab, bias_tab, *to_cast, c, w_ada, b_ada.reshape(1, -1))


def _tile_sizes(seq, dff):
    tm_ffn = next(t for t in (1024, 512, 256, 128) if seq % t == 0)
    tf = next(t for t in (512, 256, 128) if dff % t == 0)
    tm_mix = next(t for t in (512, 256, 128) if seq % t == 0)
    return tm_ffn, tf, tm_mix


def kernel(x, c, w_ada, b_ada, g_ffn1, w1_ffn1, w3_ffn1, w2_ffn1, g_mix, w_in, spatial_w, spatial_b,
           g_v, g_q, g_k, sinks, rel_bias, w_out, g_ffn2, w1_ffn2, w3_ffn2, w2_ffn2):
    bsz, seq, d = x.shape
    depth = w_ada.shape[0]
    dff = w1_ffn1.shape[-1]
    assert seq % BLOCK == 0 and w_in.shape[-1] == 2 * A_WIDTH + QKV_WIDTH
    tm_ffn, tf, tm_mix = _tile_sizes(seq, dff)
    mod_rows_early = 6
    ada_tn = next(t for t in (1024, 512, 256, 128) if (mod_rows_early * d) % t == 0)

    bias_tab = _bias_table(rel_bias)
    x2 = x.reshape(bsz * seq, d)
    for l in range(depth):
        mod3 = _ada_mod(c, w_ada[l], b_ada[l], tn=ada_tn,
                        n=mod_rows_early * d).reshape(bsz, mod_rows_early, d)
        head_tile, w1a, w3a, w2a = _ffn_head(x2, mod3, g_ffn1[l], w1_ffn1[l], w3_ffn1[l], w2_ffn1[l],
                                             mod_row=0, tm=tm_ffn)
        x2 = _ffn(x2, mod3, g_ffn1[l], w1a, w3a, w2a, mod_row=0, seq=seq, tm=tm_ffn, tf=tf,
                  head_tile=head_tile)
        w_uv, w_qkvt = _w_in_prep(w_in[l])
        x2, w1b, w3b, w2b, mod_tail = _mix(
            x2, mod3, g_mix[l], w_uv, w_qkvt, w_out[l].astype(BF16),
            spatial_w[l],
            jnp.repeat(spatial_b[l].T, A_HEAD_DIM, axis=1),
            g_v[l].reshape(1, A_WIDTH),
            jnp.broadcast_to(g_q[l][:, None], (HEAD_DIM, BLOCK)),
            jnp.broadcast_to(g_k[l][:, None], (HEAD_DIM, BLOCK)),
            sinks[l], bias_tab,
            (w1_ffn2[l], w3_ffn2[l], w2_ffn2[l]), c, w_ada[l], b_ada[l], mod_rows_early * d,
            seq=seq, tm=tm_mix)
        x2 = _ffn(x2, mod_tail.reshape(bsz, N_MOD - mod_rows_early, d), g_ffn2[l], w1b, w3b, w2b,
                  mod_row=0, seq=seq, tm=tm_ffn, tf=tf)
    return x2.reshape(bsz, seq, d)
```

```python
import functools
import math

import jax
import jax.numpy as jnp
import numpy as np
from jax import lax
from jax.experimental import pallas as pl
from jax.experimental.pallas import tpu as pltpu

BLOCK = 128
A_HEADS = 8
A_HEAD_DIM = 128
A_WIDTH = A_HEADS * A_HEAD_DIM
B_HEADS = 16
B_KV_HEADS = 2
GROUP = B_HEADS // B_KV_HEADS
HEAD_DIM = 64
B_WIDTH = B_HEADS * HEAD_DIM
KV_WIDTH = B_KV_HEADS * HEAD_DIM
QKV_WIDTH = B_WIDTH + 2 * KV_WIDTH
N_BUCKETS = 32
MAX_DISTANCE = 128
N_MOD = 9
EPS = 1e-6
MASKED = float("-inf")

V7X_VMEM_LIMIT_BYTES = 60000 * 1024
V7X_VMEM_CAPACITY_BYTES = 64 * 1024 * 1024
BIG_VMEM_LIMIT_BYTES = V7X_VMEM_CAPACITY_BYTES - 1024 * 1024

F32 = jnp.float32
BF16 = jnp.bfloat16


def _dot(a, b):
    return jnp.dot(a, b, preferred_element_type=F32)


def _dot_nt(a, b):
    return lax.dot_general(a, b, (((1,), (1,)), ((), ())), preferred_element_type=F32)


def _dot_tn(a, b):
    return lax.dot_general(a, b, (((0,), (0,)), ((), ())), preferred_element_type=F32)


NORM_ROWS = 16


def _norm_modulate_into(h_ref, x_ref, g, shift, scale):
    gain = g * (1.0 + scale)
    for r0 in range(0, x_ref.shape[0], NORM_ROWS):
        rows = slice(r0, r0 + NORM_ROWS)
        x = x_ref[rows, :]
        ms = jnp.mean(x * x, axis=-1, keepdims=True)
        h_ref[rows, :] = (x * lax.rsqrt(ms + EPS) * gain + shift).astype(BF16)


def _ada_kernel(c_ref, w_ref, b_ref, o_ref):
    c = c_ref[...]
    c_act = (c * jax.nn.sigmoid(c)).astype(BF16)
    o_ref[...] = _dot(c_act, w_ref[...].astype(BF16)) + b_ref[...]


def _ada_mod(c, w_ada, b_ada, *, tn):
    bsz, d = c.shape
    n = w_ada.shape[1]
    return pl.pallas_call(
        _ada_kernel,
        grid=(n // tn,),
        in_specs=[
            pl.BlockSpec((bsz, d), lambda j: (0, 0)),
            pl.BlockSpec((d, tn), lambda j: (0, j)),
            pl.BlockSpec((1, tn), lambda j: (0, j)),
        ],
        out_specs=pl.BlockSpec((bsz, tn), lambda j: (0, j)),
        out_shape=jax.ShapeDtypeStruct((bsz, n), F32),
        compiler_params=pltpu.CompilerParams(
            dimension_semantics=("arbitrary",), vmem_limit_bytes=V7X_VMEM_LIMIT_BYTES),
        name="ada_mod",
    )(c, w_ada, b_ada.reshape(1, n))


def _ffn_kernel(x_hbm, mod_ref, g_ref, w1a_ref, w3a_ref, w2a_ref, w1b_ref, w3b_ref, w2b_ref, head_hbm,
                o_ref, x_buf, h_ref, x_sem, head_sem, *, mod_row, has_head):
    i = pl.program_id(0)
    f = pl.program_id(1)
    tm = x_buf.shape[0]
    first = 1 if has_head else 0
    computes = i >= first

    def x_copy(tile):
        return pltpu.make_async_copy(x_hbm.at[pl.ds(tile * tm, tm), :], x_buf, x_sem)

    def chunk(w1_ref, w3_ref, w2_ref):
        h = h_ref[...]
        a = _dot(h, w1_ref[...])
        b = _dot(h, w3_ref[...])
        p = (a * jax.nn.sigmoid(a) * b).astype(BF16)
        return (0.5 * mod_ref[0, mod_row + 2:mod_row + 3, :]) * _dot(p, w2_ref[...])

    if has_head:
        @pl.when(jnp.logical_and(i == 0, f == 0))
        def _():
            head_copy = pltpu.make_async_copy(head_hbm, o_ref, head_sem)
            head_copy.start()
            head_copy.wait()

    @pl.when(jnp.logical_and(f == 0, computes))
    def _():
        if not has_head:
            @pl.when(i == 0)
            def _():
                x_copy(0).start()

        x_copy(i).wait()
        _norm_modulate_into(h_ref, x_buf, g_ref[...], mod_ref[0, mod_row:mod_row + 1, :],
                            mod_ref[0, mod_row + 1:mod_row + 2, :])
        o_ref[...] = x_buf[...] + chunk(w1a_ref, w3a_ref, w2a_ref)

    @pl.when(jnp.logical_and(f == 1, i + 1 < pl.num_programs(0)))
    def _():
        x_copy(i + 1).start()

    @pl.when(jnp.logical_and(f != 0, computes))
    def _():
        h = h_ref[...]
        half_gate = 0.5 * mod_ref[0, mod_row + 2:mod_row + 3, :]
        a1 = _dot(h, w1a_ref[...])
        b1 = _dot(h, w3a_ref[...])
        a2 = _dot(h, w1b_ref[...])
        p1 = (a1 * jax.nn.sigmoid(a1) * b1).astype(BF16)
        b2 = _dot(h, w3b_ref[...])
        o_ref[...] += half_gate * _dot(p1, w2a_ref[...])
        p2 = (a2 * jax.nn.sigmoid(a2) * b2).astype(BF16)
        o_ref[...] += half_gate * _dot(p2, w2b_ref[...])


FFN_HEAD_TF = 256


def _ffn_head_kernel(x_ref, mod_ref, g_ref, w1_ref, w3_ref, w2_ref, o_ref, w1b_ref, w3b_ref, w2b_ref, h_ref,
                     *, mod_row):
    f = pl.program_id(0)

    def chunk():
        w1b_ref[...] = w1_ref[...].astype(BF16)
        w3b_ref[...] = w3_ref[...].astype(BF16)
        w2b_ref[...] = w2_ref[...].astype(BF16)
        h = h_ref[...]
        a = _dot(h, w1b_ref[...])
        b = _dot(h, w3b_ref[...])
        p = (a * jax.nn.sigmoid(a) * b).astype(BF16)
        return (0.5 * mod_ref[0, mod_row + 2:mod_row + 3, :]) * _dot(p, w2b_ref[...])

    @pl.when(f == 0)
    def _():
        _norm_modulate_into(h_ref, x_ref, g_ref[...], mod_ref[0, mod_row:mod_row + 1, :],
                            mod_ref[0, mod_row + 1:mod_row + 2, :])
        o_ref[...] = x_ref[...] + chunk()

    @pl.when(f != 0)
    def _():
        o_ref[...] += chunk()


def _ffn_head(x2, mod3, g, w1, w3, w2, *, mod_row, tm):
    n, d = x2.shape
    dff = w1.shape[1]
    tf = FFN_HEAD_TF
    up = pl.BlockSpec((d, tf), lambda f: (0, f))
    down = pl.BlockSpec((tf, d), lambda f: (f, 0))
    first_tile = pl.BlockSpec((tm, d), lambda f: (0, 0))
    return pl.pallas_call(
        functools.partial(_ffn_head_kernel, mod_row=mod_row),
        grid=(dff // tf,),
        in_specs=[
            pl.BlockSpec((tm, d), lambda f: (0, 0), pipeline_mode=pl.Buffered(1)),
            pl.BlockSpec((1, N_MOD, d), lambda f: (0, 0, 0)),
            pl.BlockSpec((1, d), lambda f: (0, 0)),
            up, up, down,
        ],
        out_specs=[first_tile, up, up, down],
        out_shape=[jax.ShapeDtypeStruct((tm, d), F32), jax.ShapeDtypeStruct(w1.shape, BF16),
                   jax.ShapeDtypeStruct(w3.shape, BF16), jax.ShapeDtypeStruct(w2.shape, BF16)],
        scratch_shapes=[pltpu.VMEM((tm, d), BF16)],
        compiler_params=pltpu.CompilerParams(
            dimension_semantics=("arbitrary",), vmem_limit_bytes=BIG_VMEM_LIMIT_BYTES),
        name=f"ffn_head_mod{mod_row}",
    )(x2, mod3, g.reshape(1, d), w1, w3, w2)


def _ffn(x2, mod3, g, w1, w3, w2, *, mod_row, seq, tm, tf, head_tile=None):
    n, d = x2.shape
    nf = w1.shape[1] // tf
    assert nf % 2 == 1 and nf >= 3, "one chunk at step 0, then pairs"
    has_head = head_tile is not None
    tiles_per_seq = seq // tm
    first_of_pair = lambda i, f: jnp.where(jnp.logical_and(has_head, i == 0), 0, jnp.maximum(2 * f - 1, 0))
    second_of_pair = lambda i, f: jnp.where(jnp.logical_and(has_head, i == 0), 2, jnp.maximum(2 * f, 2))
    up_a = pl.BlockSpec((d, tf), lambda i, f: (0, first_of_pair(i, f)))
    up_b = pl.BlockSpec((d, tf), lambda i, f: (0, second_of_pair(i, f)))
    down_a = pl.BlockSpec((tf, d), lambda i, f: (first_of_pair(i, f), 0))
    down_b = pl.BlockSpec((tf, d), lambda i, f: (second_of_pair(i, f), 0))
    return pl.pallas_call(
        functools.partial(_ffn_kernel, mod_row=mod_row, has_head=has_head),
        grid=(n // tm, (nf + 1) // 2),
        in_specs=[
            pl.BlockSpec(memory_space=pl.ANY),
            pl.BlockSpec((1, N_MOD, d), lambda i, f: (i // tiles_per_seq, 0, 0)),
            pl.BlockSpec((1, d), lambda i, f: (0, 0)),
            up_a, up_a, down_a, up_b, up_b, down_b,
            pl.BlockSpec(memory_space=pl.ANY),
        ],
        out_specs=pl.BlockSpec((tm, d), lambda i, f: (i, 0)),
        out_shape=jax.ShapeDtypeStruct((n, d), F32),
        scratch_shapes=[pltpu.VMEM((tm, d), F32), pltpu.VMEM((tm, d), BF16),
                        pltpu.SemaphoreType.DMA(()), pltpu.SemaphoreType.DMA(())],
        compiler_params=pltpu.CompilerParams(
            dimension_semantics=("arbitrary", "arbitrary"),
            vmem_limit_bytes=BIG_VMEM_LIMIT_BYTES),
        name=f"ffn_mod{mod_row}",
    )(x2, mod3, g.reshape(1, d), w1, w3, w2, w1, w3, w2, head_tile if has_head else x2)


def _t5_bucket_table():
    kj = np.arange(2 * BLOCK)[:, None]
    qi = np.arange(BLOCK)[None, :]
    dist = qi + BLOCK - kj
    in_window = (dist >= 0) & (dist < BLOCK)
    n = np.clip(dist, 0, None)
    max_exact = N_BUCKETS // 2
    nf = np.maximum(n, 1).astype(np.float32)
    large = max_exact + (np.log(nf / np.float32(max_exact)) / np.float32(math.log(MAX_DISTANCE / max_exact))
                         * np.float32(N_BUCKETS - max_exact)).astype(np.int32)
    large = np.minimum(large, N_BUCKETS - 1)
    bucket = np.where(n < max_exact, n, large)
    return np.where(in_window, bucket, -1).astype(np.int32)


def _bias_kernel(rb_ref, bucket_ref, o_ref):
    bucket = bucket_ref[...]
    prev_rows = lax.broadcasted_iota(jnp.int32, bucket.shape, 0) < BLOCK
    for head in range(B_HEADS):
        acc = jnp.full(bucket.shape, MASKED, F32)
        for b in range(N_BUCKETS):
            acc = jnp.where(bucket == b, rb_ref[b, head], acc)
        j, hl = divmod(head, GROUP)
        o_ref[0, j, :, hl * BLOCK:(hl + 1) * BLOCK] = acc
        o_ref[1, j, :, hl * BLOCK:(hl + 1) * BLOCK] = jnp.where(prev_rows, MASKED, acc)


def _bias_table(rel_bias):
    return pl.pallas_call(
        _bias_kernel,
        in_specs=[
            pl.BlockSpec(memory_space=pltpu.SMEM),
            pl.BlockSpec((2 * BLOCK, BLOCK), lambda: (0, 0)),
        ],
        out_specs=pl.BlockSpec((2, B_KV_HEADS, 2 * BLOCK, GROUP * BLOCK), lambda: (0, 0, 0, 0)),
        out_shape=jax.ShapeDtypeStruct((2, B_KV_HEADS, 2 * BLOCK, GROUP * BLOCK), F32),
        name="rel_bias_table",
    )(rel_bias, jnp.asarray(_t5_bucket_table()))


W_IN_PREP_ROWS = 256


def _w_in_prep_kernel(w_ref, uv_ref, qkvt_ref):
    w = w_ref[...]
    uv_ref[...] = w[:, :2 * A_WIDTH].astype(BF16)
    qkvt_ref[...] = w[:, 2 * A_WIDTH:].T.astype(BF16)


def _w_in_prep(w_in):
    d, cols = w_in.shape
    rows = min(W_IN_PREP_ROWS, d)
    return pl.pallas_call(
        _w_in_prep_kernel,
        grid=(d // rows,),
        in_specs=[pl.BlockSpec((rows, cols), lambda r: (r, 0))],
        out_specs=[pl.BlockSpec((rows, 2 * A_WIDTH), lambda r: (r, 0)),
                   pl.BlockSpec((QKV_WIDTH, rows), lambda r: (0, r))],
        out_shape=[jax.ShapeDtypeStruct((d, 2 * A_WIDTH), BF16),
                   jax.ShapeDtypeStruct((QKV_WIDTH, d), BF16)],
        compiler_params=pltpu.CompilerParams(dimension_semantics=("arbitrary",)),
        name="w_in_prep",
    )(w_in)


def _gelu(x):
    return 0.5 * x * (1.0 + lax.erf(x * (1.0 / math.sqrt(2.0))))


def _rms_rows(xt, gain):
    ms = jnp.mean(xt * xt, axis=0, keepdims=True)
    return xt * lax.rsqrt(ms + EPS) * gain


def _mix_kernel(sinks_ref, x_ref, mod_ref, g_ref, wuv_ref, wqkvt_ref, wout_ref, sw_ref, sb_ref,
                gv_ref, gq_ref, gk_ref, bias_ref, cast_a_ref, cast_b_ref, cast_c_ref,
                o_ref, cast_a_out, cast_b_out, cast_c_out, h_ref, kprev_ref, vprev_ref, *, tiles_per_seq):
    tm = x_ref.shape[0]
    nblk = tm // BLOCK
    first = pl.program_id(0) % tiles_per_seq == 0

    @pl.when(first)
    def _():
        kprev_ref[...] = jnp.zeros_like(kprev_ref)
        vprev_ref[...] = jnp.zeros_like(vprev_ref)

    _norm_modulate_into(h_ref, x_ref, g_ref[...], mod_ref[0, 3:4, :], mod_ref[0, 4:5, :])
    h = h_ref[...]

    zuv = _dot(h, wuv_ref[...])

    for src, dst in ((cast_a_ref, cast_a_out), (cast_b_ref, cast_b_out), (cast_c_ref, cast_c_out)):
        dst[...] = src[...].astype(BF16)

    u = _gelu(zuv[:, :A_WIDTH])
    v = _gelu(zuv[:, A_WIDTH:])
    row = lax.broadcasted_iota(jnp.int32, (BLOCK, BLOCK), 0)
    col = lax.broadcasted_iota(jnp.int32, (BLOCK, BLOCK), 1)
    causal = row >= col
    ya_heads = []
    for hd in range(A_HEADS):
        cols = slice(hd * A_HEAD_DIM, (hd + 1) * A_HEAD_DIM)
        vh = v[:, cols]
        ms = jnp.mean(vh * vh, axis=-1, keepdims=True)
        vn = (vh * lax.rsqrt(ms + EPS) * gv_ref[:, cols]).astype(BF16)
        w_tril = jnp.where(causal, sw_ref[hd], 0.0).astype(BF16)
        mixed = [_dot(w_tril, vn[c * BLOCK:(c + 1) * BLOCK]) + sb_ref[:, cols] for c in range(nblk)]
        ya_heads.append(u[:, cols] * jnp.concatenate(mixed, axis=0))
    ya = jnp.concatenate(ya_heads, axis=1).astype(BF16)

    gate = mod_ref[0, 5:6, :]
    d_model = o_ref.shape[1]
    slab = max(d_model // (nblk * B_KV_HEADS), BLOCK)
    n_slabs = d_model // slab

    def out_proj_a(unit):
        if unit < n_slabs:
            cols = slice(unit * slab, (unit + 1) * slab)
            o_ref[:, cols] = x_ref[:, cols] + gate[:, cols] * _dot(ya, wout_ref[:A_WIDTH, cols])

    zt = _dot_nt(wqkvt_ref[...], h)
    out_proj_a(0)
    gq = gq_ref[...] * (HEAD_DIM ** -0.5)
    qn = [_rms_rows(zt[hd * HEAD_DIM:(hd + 1) * HEAD_DIM], jnp.tile(gq, (1, nblk))).astype(BF16)
          for hd in range(B_HEADS)]
    kn, vv = [], []
    for j in range(B_KV_HEADS):
        k_rows = slice(B_WIDTH + j * HEAD_DIM, B_WIDTH + (j + 1) * HEAD_DIM)
        v_rows = slice(B_WIDTH + KV_WIDTH + j * HEAD_DIM, B_WIDTH + KV_WIDTH + (j + 1) * HEAD_DIM)
        kn.append(_rms_rows(zt[k_rows], jnp.tile(gk_ref[...], (1, nblk))).astype(BF16))
        vv.append(zt[v_rows].astype(BF16))

    variant = jnp.where(first, 1, 0)
    yb_blocks = []
    for c in range(nblk):
        tok = slice(c * BLOCK, (c + 1) * BLOCK)
        head_rows = [None] * B_HEADS
        for j in range(B_KV_HEADS):
            if c == 0:
                k_prev, v_prev = kprev_ref[j], vprev_ref[j]
                bias = bias_ref[variant, j]
            else:
                prev = slice((c - 1) * BLOCK, c * BLOCK)
                k_prev, v_prev = kn[j][:, prev], vv[j][:, prev]
                bias = bias_ref[0, j]
            k_band = jnp.concatenate([k_prev, kn[j][:, tok]], axis=1)
            v_band = jnp.concatenate([v_prev, vv[j][:, tok]], axis=1)
            q_grp = jnp.concatenate([qn[j * GROUP + hl][:, tok] for hl in range(GROUP)], axis=1)
            s = _dot_tn(k_band, q_grp) + bias
            out_proj_a(c * B_KV_HEADS + j + 1)
            sink = jnp.concatenate(
                [jnp.full((1, BLOCK), sinks_ref[j * GROUP + hl], F32) for hl in range(GROUP)], axis=1)
            m = jnp.maximum(jnp.max(s, axis=0, keepdims=True), sink)
            p = jnp.exp(s - m)
            denom = jnp.sum(p, axis=0, keepdims=True) + jnp.exp(sink - m)
            o_t = _dot(v_band, p.astype(BF16)) / denom
            for hl in range(GROUP):
                head_rows[j * GROUP + hl] = o_t[:, hl * BLOCK:(hl + 1) * BLOCK]
        yb_blocks.append(jnp.concatenate(head_rows, axis=0))
    yb = jnp.concatenate(yb_blocks, axis=1).T

    last = slice((nblk - 1) * BLOCK, nblk * BLOCK)
    for j in range(B_KV_HEADS):
        kprev_ref[j] = kn[j][:, last]
        vprev_ref[j] = vv[j][:, last]

    o_ref[...] += gate * _dot(yb.astype(BF16), wout_ref[A_WIDTH:, :])


def _cast_block_spec(shape, steps):
    rows, cols = shape
    for row_parts in range(steps, 0, -1):
        col_parts = steps // row_parts
        if (row_parts * col_parts == steps and rows % (16 * row_parts) == 0
                and cols % (128 * col_parts) == 0):
            return pl.BlockSpec((rows // row_parts, cols // col_parts),
                                lambda i: (i // col_parts, i % col_parts))
    raise ValueError(f"no {steps}-step tiling of {shape}")


def _mix(x2, mod3, g_mix, w_uv, w_qkvt, w_out, spatial_w, sb_full, gv_row, gq_tab, gk_tab, sinks,
         bias_tab, to_cast, *, seq, tm):
    n, d = x2.shape
    tiles_per_seq = seq // tm
    steps = n // tm
    const2 = lambda i: (0, 0)
    cast_specs = [_cast_block_spec(w.shape, steps) for w in to_cast]
    return pl.pallas_call(
        functools.partial(_mix_kernel, tiles_per_seq=tiles_per_seq),
        grid=(n // tm,),
        in_specs=[
            pl.BlockSpec(memory_space=pltpu.SMEM),
            pl.BlockSpec((tm, d), lambda i: (i, 0)),
            pl.BlockSpec((1, N_MOD, d), lambda i: (i // tiles_per_seq, 0, 0)),
            pl.BlockSpec((1, d), const2),
            pl.BlockSpec(w_uv.shape, const2, pipeline_mode=pl.Buffered(1)),
            pl.BlockSpec(w_qkvt.shape, const2, pipeline_mode=pl.Buffered(1)),
            pl.BlockSpec(w_out.shape, const2, pipeline_mode=pl.Buffered(1)),
            pl.BlockSpec(spatial_w.shape, lambda i: (0, 0, 0), pipeline_mode=pl.Buffered(1)),
            pl.BlockSpec(sb_full.shape, const2, pipeline_mode=pl.Buffered(1)),
            pl.BlockSpec(gv_row.shape, const2),
            pl.BlockSpec(gq_tab.shape, const2),
            pl.BlockSpec(gk_tab.shape, const2),
            pl.BlockSpec(bias_tab.shape, lambda i: (0, 0, 0, 0), pipeline_mode=pl.Buffered(1)),
            *cast_specs,
        ],
        out_specs=[pl.BlockSpec((tm, d), lambda i: (i, 0)), *cast_specs],
        scratch_shapes=[pltpu.VMEM((tm, d), BF16),
                        pltpu.VMEM((B_KV_HEADS, HEAD_DIM, BLOCK), BF16),
                        pltpu.VMEM((B_KV_HEADS, HEAD_DIM, BLOCK), BF16)],
        out_shape=[jax.ShapeDtypeStruct((n, d), F32)]
        + [jax.ShapeDtypeStruct(w.shape, BF16) for w in to_cast],
        compiler_params=pltpu.CompilerParams(
            dimension_semantics=("arbitrary",), vmem_limit_bytes=BIG_VMEM_LIMIT_BYTES,
            allow_input_fusion=[k in (6, 8, 10, 11) for k in range(13 + len(to_cast))]),
        name="token_mix",
    )(sinks, x2, mod3, g_mix.reshape(1, d), w_uv, w_qkvt, w_out, spatial_w, sb_full, gv_row,
      gq_tab, gk_tab, bias_tab, *to_cast)


def _tile_sizes(seq, dff):
    tm_ffn = next(t for t in (1024, 512, 256, 128) if seq % t == 0)
    tf = next(t for t in (512, 256, 128) if dff % t == 0)
    tm_mix = next(t for t in (512, 256, 128) if seq % t == 0)
    return tm_ffn, tf, tm_mix


def kernel(x, c, w_ada, b_ada, g_ffn1, w1_ffn1, w3_ffn1, w2_ffn1, g_mix, w_in, spatial_w, spatial_b,
           g_v, g_q, g_k, sinks, rel_bias, w_out, g_ffn2, w1_ffn2, w3_ffn2, w2_ffn2):
    bsz, seq, d = x.shape
    depth = w_ada.shape[0]
    dff = w1_ffn1.shape[-1]
    assert seq % BLOCK == 0 and w_in.shape[-1] == 2 * A_WIDTH + QKV_WIDTH
    tm_ffn, tf, tm_mix = _tile_sizes(seq, dff)
    ada_tn = next(t for t in (1024, 512, 256, 128) if (N_MOD * d) % t == 0)

    bias_tab = _bias_table(rel_bias)
    x2 = x.reshape(bsz * seq, d)
    for l in range(depth):
        mod3 = _ada_mod(c, w_ada[l], b_ada[l], tn=ada_tn).reshape(bsz, N_MOD, d)
        head_tile, w1a, w3a, w2a = _ffn_head(x2, mod3, g_ffn1[l], w1_ffn1[l], w3_ffn1[l], w2_ffn1[l],
                                             mod_row=0, tm=tm_ffn)
        x2 = _ffn(x2, mod3, g_ffn1[l], w1a, w3a, w2a, mod_row=0, seq=seq, tm=tm_ffn, tf=tf,
                  head_tile=head_tile)
        w_uv, w_qkvt = _w_in_prep(w_in[l])
        x2, w1b, w3b, w2b = _mix(
            x2, mod3, g_mix[l], w_uv, w_qkvt, w_out[l].astype(BF16),
            spatial_w[l],
            jnp.repeat(spatial_b[l].T, A_HEAD_DIM, axis=1),
            g_v[l].reshape(1, A_WIDTH),
            jnp.broadcast_to(g_q[l][:, None], (HEAD_DIM, BLOCK)),
            jnp.broadcast_to(g_k[l][:, None], (HEAD_DIM, BLOCK)),
            sinks[l], bias_tab,
            (w1_ffn2[l], w3_ffn2[l], w2_ffn2[l]), seq=seq, tm=tm_mix)
        x2 = _ffn(x2, mod3, g_ffn2[l], w1b, w3b, w2b,
                  mod_row=6, seq=seq, tm=tm_ffn, tf=tf)
    return x2.reshape(bsz, seq, d)
```
